```python
import math
import jax, jax.numpy as jnp
from jax import lax
import numpy as np

D_MODEL = 2048
BATCH = 2
SEQ = 8192
DEPTH = 2

N_Q_HEADS = 8
N_KV_HEADS = 2
HEAD_DIM = 128
Q_PER_KV = N_Q_HEADS // N_KV_HEADS
WINDOW = 128
ATTN_BLOCK = 128
ATTN_W = N_Q_HEADS * HEAD_DIM
KV_W = N_KV_HEADS * HEAD_DIM

SG_GROUPS = 4
SG_CHUNK = 128
SG_GROUP_CH = 128
SG_W = SG_GROUPS * SG_GROUP_CH

SSM_GROUP_CH = 16
SSM_GROUPS = 32
SSM_W = SSM_GROUPS * SSM_GROUP_CH
SSM_STATE = 64
DT_MIN = 0.001
DT_MAX = 0.1

N_BRANCH = 3
GATE_W = N_BRANCH * D_MODEL
SPLIT_POINTS = (
    ATTN_W,
    ATTN_W + KV_W,
    ATTN_W + 2 * KV_W,
    ATTN_W + 2 * KV_W + SG_W,
    ATTN_W + 2 * KV_W + 2 * SG_W,
    ATTN_W + 2 * KV_W + 2 * SG_W + SSM_W,
)
IN_W = SPLIT_POINTS[-1] + GATE_W

D_FF = 5632
CONV_W = 3
EPS = 1e-6

kernel_name = "hybrid_swa_gmlp_s5_convffn"


def rms_norm(x, g):
    xf = x.astype(jnp.float32)
    y = xf * lax.rsqrt(jnp.mean(xf * xf, axis=-1, keepdims=True) + EPS)
    return (y * g.astype(jnp.float32)).astype(x.dtype)


def layer_norm(x, g, b):
    xf = x.astype(jnp.float32)
    xc = xf - jnp.mean(xf, axis=-1, keepdims=True)
    var = jnp.mean(xc * xc, axis=-1, keepdims=True)
    y = xc * lax.rsqrt(var + EPS) * g.astype(jnp.float32) + b.astype(jnp.float32)
    return y.astype(x.dtype)


def alibi_slopes():
    return 2.0 ** (-8.0 * jnp.arange(1, N_Q_HEADS + 1, dtype=jnp.float32) / N_Q_HEADS)


def _with_prev_block(t):
    prev = jnp.pad(t, ((0, 0), (1, 0), (0, 0), (0, 0), (0, 0)))[:, :-1]
    return jnp.concatenate([prev, t], axis=2)


def sliding_window_attention(q, k, v, q_gain, k_gain, sinks):
    B, L = q.shape[0], q.shape[1]
    nb = L // ATTN_BLOCK
    q = rms_norm(q.reshape(B, nb, ATTN_BLOCK, N_KV_HEADS, Q_PER_KV, HEAD_DIM), q_gain)
    k = rms_norm(k.reshape(B, nb, ATTN_BLOCK, N_KV_HEADS, HEAD_DIM), k_gain)
    v = v.reshape(B, nb, ATTN_BLOCK, N_KV_HEADS, HEAD_DIM)
    kw = _with_prev_block(k)
    vw = _with_prev_block(v)
    s = jnp.einsum("bnqhgd,bnkhd->bhgnqk", q, kw,
                   preferred_element_type=jnp.float32) * (HEAD_DIM ** -0.5)
    q_idx = jnp.arange(ATTN_BLOCK) + ATTN_BLOCK
    k_idx = jnp.arange(2 * ATTN_BLOCK)
    dist = q_idx[:, None] - k_idx[None, :]
    k_abs = jnp.arange(nb)[:, None] * ATTN_BLOCK + k_idx[None, :] - ATTN_BLOCK
    valid = ((dist >= 0) & (dist < WINDOW))[None] & (k_abs >= 0)[:, None, :]
    slopes = alibi_slopes().reshape(N_KV_HEADS, Q_PER_KV, 1, 1, 1)
    s = s - slopes * dist.astype(jnp.float32)
    s = jnp.where(valid, s, -jnp.inf)
    sink = sinks.astype(jnp.float32).reshape(N_KV_HEADS, Q_PER_KV, 1, 1, 1)
    m = jnp.maximum(jnp.max(s, axis=-1, keepdims=True), sink)
    p = jnp.exp(s - m)
    probs = p / (jnp.sum(p, axis=-1, keepdims=True) + jnp.exp(sink - m))
    o = jnp.einsum("bhgnqk,bnkhd->bnqhgd", probs.astype(vw.dtype), vw)
    return o.reshape(B, L, ATTN_W)


def spatial_gating(z_u, z_v, ln_g, ln_b, w_s, b_s):
    B, L = z_u.shape[0], z_u.shape[1]
    nc = L // SG_CHUNK
    zv = layer_norm(z_v.reshape(B, nc, SG_CHUNK, SG_GROUPS, SG_GROUP_CH), ln_g, ln_b)
    w = w_s * jnp.tril(jnp.ones((SG_CHUNK, SG_CHUNK), dtype=w_s.dtype))
    mixed = jnp.einsum("gts,bnsgc->bntgc", w, zv) + b_s.T[:, :, None]
    return z_u * mixed.reshape(B, L, SG_W)


def s5_glu(u, a_re, a_im, log_dt, b_re, b_im, c_re, c_im, d_skip, w_glu, b_glu):
    B, L = u.shape[0], u.shape[1]
    f32 = jnp.float32
    uf = u.astype(f32).reshape(B, L, SSM_GROUPS, SSM_GROUP_CH)
    lam = lax.complex(a_re.astype(f32), a_im.astype(f32))
    dt = jnp.exp(log_dt.astype(f32))[:, None]
    a_bar = jnp.exp(lam * dt)
    b_mat = lax.complex(b_re.astype(f32), b_im.astype(f32))
    b_bar = ((a_bar - 1.0) / lam)[..., None] * b_mat
    bu = jnp.einsum("gpc,blgc->blgp", b_bar, uf.astype(jnp.complex64))
    a_seq = jnp.broadcast_to(a_bar, bu.shape)

    def combine(left, right):
        a1, x1 = left
        a2, x2 = right
        return a1 * a2, a2 * x1 + x2

    _, h = lax.associative_scan(combine, (a_seq, bu), axis=1)
    c_mat = lax.complex(c_re.astype(f32), c_im.astype(f32))
    y = jnp.real(jnp.einsum("gcp,blgp->blgc", c_mat, h))
    y = y + d_skip.astype(f32).reshape(SSM_GROUPS, SSM_GROUP_CH) * uf
    y = jax.nn.gelu(y.reshape(B, L, SSM_W))
    out = y * jax.nn.sigmoid(y @ w_glu.astype(f32) + b_glu.astype(f32))
    return out.astype(u.dtype)


def conv_ffn(h, w_up, conv_w, conv_b, w_down):
    L = h.shape[1]
    up = h @ w_up
    up_pad = jnp.pad(up, ((0, 0), (CONV_W - 1, 0), (0, 0)))
    conv = conv_b + conv_w[0] * up
    for lag in range(1, CONV_W):
        conv = conv + conv_w[lag] * up_pad[:, CONV_W - 1 - lag:CONV_W - 1 - lag + L]
    gate, val = jnp.split(conv, 2, axis=-1)
    return (jax.nn.gelu(gate) * val) @ w_down


def setup_inputs(seed: int = 0) -> dict:
    key = jax.random.key(seed)
    ks = jax.random.split(key, 32)
    f32 = jnp.float32

    def nrm(k, shape, scale):
        return scale * jax.random.normal(k, shape, f32)

    def gain(k, shape):
        return 1.0 + 0.02 * jax.random.normal(k, shape, f32)

    P = SSM_STATE
    n_idx = jnp.arange(P, dtype=f32)
    return {
        "x": nrm(ks[0], (BATCH, SEQ, D_MODEL), 1.0),
        "norm1_g": gain(ks[1], (DEPTH, D_MODEL)),
        "w_in": nrm(ks[2], (DEPTH, D_MODEL, IN_W), D_MODEL ** -0.5),
        "b_gate": nrm(ks[3], (DEPTH, GATE_W), 0.02),
        "q_norm_g": gain(ks[4], (DEPTH, HEAD_DIM)),
        "k_norm_g": gain(ks[5], (DEPTH, HEAD_DIM)),
        "attn_sinks": nrm(ks[6], (DEPTH, N_Q_HEADS), 0.5),
        "sg_ln_g": gain(ks[7], (DEPTH, SG_GROUPS, SG_GROUP_CH)),
        "sg_ln_b": nrm(ks[8], (DEPTH, SG_GROUPS, SG_GROUP_CH), 0.02),
        "sg_w": nrm(ks[9], (DEPTH, SG_GROUPS, SG_CHUNK, SG_CHUNK), 0.5 * SG_CHUNK ** -0.5),
        "sg_b": 1.0 + nrm(ks[10], (DEPTH, SG_GROUPS, SG_CHUNK), 0.02),
        "ssm_a_re": -0.5 + nrm(ks[11], (DEPTH, SSM_GROUPS, P), 0.01),
        "ssm_a_im": jnp.pi * n_idx + nrm(ks[12], (DEPTH, SSM_GROUPS, P), 0.01),
        "ssm_log_dt": jax.random.uniform(ks[13], (DEPTH, SSM_GROUPS), f32,
                                         math.log(DT_MIN), math.log(DT_MAX)),
        "ssm_b_re": nrm(ks[14], (DEPTH, SSM_GROUPS, P, SSM_GROUP_CH), (2 * SSM_GROUP_CH) ** -0.5),
        "ssm_b_im": nrm(ks[15], (DEPTH, SSM_GROUPS, P, SSM_GROUP_CH), (2 * SSM_GROUP_CH) ** -0.5),
        "ssm_c_re": nrm(ks[16], (DEPTH, SSM_GROUPS, SSM_GROUP_CH, P), (2 * P) ** -0.5),
        "ssm_c_im": nrm(ks[17], (DEPTH, SSM_GROUPS, SSM_GROUP_CH, P), (2 * P) ** -0.5),
        "ssm_d": nrm(ks[18], (DEPTH, SSM_W), 0.5),
        "ssm_w_glu": nrm(ks[19], (DEPTH, SSM_W, SSM_W), SSM_W ** -0.5),
        "ssm_b_glu": nrm(ks[20], (DEPTH, SSM_W), 0.02),
        "w_proj_attn": nrm(ks[21], (DEPTH, ATTN_W, D_MODEL), ATTN_W ** -0.5),
        "w_proj_sg": nrm(ks[22], (DEPTH, SG_W, D_MODEL), SG_W ** -0.5),
        "w_proj_ssm": nrm(ks[23], (DEPTH, SSM_W, D_MODEL), SSM_W ** -0.5),
        "w_out": nrm(ks[24], (DEPTH, D_MODEL, D_MODEL), D_MODEL ** -0.5),
        "norm2_g": gain(ks[25], (DEPTH, D_MODEL)),
        "ffn_w_up": nrm(ks[26], (DEPTH, D_MODEL, 2 * D_FF), D_MODEL ** -0.5),
        "ffn_conv_w": nrm(ks[27], (DEPTH, CONV_W, 2 * D_FF), CONV_W ** -0.5),
        "ffn_conv_b": nrm(ks[28], (DEPTH, 2 * D_FF), 0.02),
        "ffn_w_down": nrm(ks[29], (DEPTH, D_FF, D_MODEL), D_FF ** -0.5),
    }


def reference(x, norm1_g, w_in, b_gate, q_norm_g, k_norm_g, attn_sinks, sg_ln_g, sg_ln_b,
              sg_w, sg_b, ssm_a_re, ssm_a_im, ssm_log_dt, ssm_b_re, ssm_b_im, ssm_c_re,
              ssm_c_im, ssm_d, ssm_w_glu, ssm_b_glu, w_proj_attn, w_proj_sg, w_proj_ssm,
              w_out, norm2_g, ffn_w_up, ffn_conv_w, ffn_conv_b, ffn_w_down):
    B, L = x.shape[0], x.shape[1]
    for l in range(DEPTH):
        h = rms_norm(x, norm1_g[l])
        proj = h @ w_in[l]
        q, k, v, sg_u, sg_v, s_in, gates = jnp.split(proj, SPLIT_POINTS, axis=-1)
        y_attn = sliding_window_attention(q, k, v, q_norm_g[l], k_norm_g[l], attn_sinks[l])
        y_sg = spatial_gating(jax.nn.gelu(sg_u), jax.nn.gelu(sg_v),
                              sg_ln_g[l], sg_ln_b[l], sg_w[l], sg_b[l])
        y_ssm = s5_glu(s_in, ssm_a_re[l], ssm_a_im[l], ssm_log_dt[l], ssm_b_re[l], ssm_b_im[l],
                       ssm_c_re[l], ssm_c_im[l], ssm_d[l], ssm_w_glu[l], ssm_b_glu[l])
        g = jax.nn.sigmoid(gates + b_gate[l]).reshape(B, L, N_BRANCH, D_MODEL)
        merged = (g[:, :, 0] * (y_attn @ w_proj_attn[l])
                  + g[:, :, 1] * (y_sg @ w_proj_sg[l])
                  + g[:, :, 2] * (y_ssm @ w_proj_ssm[l]))
        x = x + merged @ w_out[l]
        x = x + conv_ffn(rms_norm(x, norm2_g[l]), ffn_w_up[l], ffn_conv_w[l],
                         ffn_conv_b[l], ffn_w_down[l])
    return x
```

```python
import functools
import math

import jax
import jax.numpy as jnp
from jax import lax
from jax.experimental import pallas as pl
from jax.experimental.pallas import tpu as pltpu

F32 = jnp.float32
BF16 = jnp.bfloat16

D_MODEL = 2048
N_Q_HEADS = 8
N_KV_HEADS = 2
Q_PER_KV = N_Q_HEADS // N_KV_HEADS
HEAD_DIM = 128
BLK = 128
ATTN_W = N_Q_HEADS * HEAD_DIM
KV_W = N_KV_HEADS * HEAD_DIM
SG_GROUPS = 4
SG_W = 512
SSM_GROUPS = 32
SSM_CH = 16
SSM_W = 512
SSM_STATE = 64
SSM_SUB = 8
SSM_KT = 4
SSM_NSTATE = SSM_GROUPS * SSM_STATE
MAIN_W = ATTN_W + 2 * KV_W + 2 * SG_W + SSM_W
GATE_W = 3 * D_MODEL
D_FF = 5632
EPS = 1e-6

VMEM_LIMIT = 56 * 1024 * 1024


def _cparams(sem):
    return pltpu.CompilerParams(dimension_semantics=sem, vmem_limit_bytes=VMEM_LIMIT)


def _gelu(x):
    c = math.sqrt(2.0 / math.pi)
    return 0.5 * x * (1.0 + jnp.tanh(c * (x + 0.044715 * (x * x * x))))


def _sigmoid(x):
    return 1.0 / (1.0 + jnp.exp(-x))


def _dot(a, b):
    return jnp.dot(a, b, preferred_element_type=F32)


def _rmsnorm_kernel(x_ref, g_ref, o_ref):
    x = x_ref[...]
    ms = jnp.mean(x * x, axis=-1, keepdims=True)
    o_ref[...] = (x * lax.rsqrt(ms + EPS) * g_ref[...]).astype(o_ref.dtype)


def _rmsnorm(x, g, tm=512):
    n, d = x.shape
    return pl.pallas_call(
        _rmsnorm_kernel,
        grid=(n // tm,),
        in_specs=[pl.BlockSpec((tm, d), lambda i: (i, 0)),
                  pl.BlockSpec((1, d), lambda i: (0, 0))],
        out_specs=pl.BlockSpec((tm, d), lambda i: (i, 0)),
        out_shape=jax.ShapeDtypeStruct((n, d), BF16),
        compiler_params=_cparams(("parallel",)),
        name="rmsnorm",
    )(x, g.reshape(1, d))


def _matmul_kernel(a_ref, b_ref, o_ref):
    o_ref[...] = _dot(a_ref[...], b_ref[...]).astype(o_ref.dtype)


def _matmul(a, b, tm=1024, tn=1024, name="matmul"):
    m, k = a.shape
    _, n = b.shape
    return pl.pallas_call(
        _matmul_kernel,
        grid=(m // tm, n // tn),
        in_specs=[pl.BlockSpec((tm, k), lambda i, j: (i, 0)),
                  pl.BlockSpec((k, tn), lambda i, j: (0, j))],
        out_specs=pl.BlockSpec((tm, tn), lambda i, j: (i, j)),
        out_shape=jax.ShapeDtypeStruct((m, n), BF16),
        compiler_params=_cparams(("parallel", "parallel")),
        name=name,
    )(a, b)


def _attn_kernel(q_ref, k_ref, v_ref, kp_ref, vp_ref, qg_ref, kg_ref, slope_ref, sink_ref, o_ref,
                 *, blocks_per_step, blocks_per_seq):
    i = pl.program_id(0)
    qg = qg_ref[...] * (HEAD_DIM ** -0.5)
    kg = kg_ref[...]
    row = lax.broadcasted_iota(jnp.int32, (BLK, BLK), 0)
    col = lax.broadcasted_iota(jnp.int32, (BLK, BLK), 1)
    cur_mask = col <= row
    dist = jnp.where(cur_mask, row - col, row - col + BLK).astype(F32)
    cur4 = jnp.concatenate([cur_mask] * Q_PER_KV, axis=0)
    dist4 = jnp.concatenate([dist] * Q_PER_KV, axis=0)

    def rms(x, g):
        return x * lax.rsqrt(jnp.mean(x * x, axis=-1, keepdims=True) + EPS) * g

    for r in range(blocks_per_step):
        rows = slice(r * BLK, (r + 1) * BLK)
        if r == 0:
            has_prev = (i * blocks_per_step) % blocks_per_seq != 0
            k_prev_raw = kp_ref[...]
            v_prev = vp_ref[...]
        else:
            has_prev = None
            prows = slice((r - 1) * BLK, r * BLK)
            k_prev_raw = k_ref[prows, :]
            v_prev = v_ref[prows, :]
        for h in range(N_KV_HEADS):
            hc = slice(h * HEAD_DIM, (h + 1) * HEAD_DIM)
            k_cur = rms(k_ref[rows, hc].astype(F32), kg)
            k_prv = rms(k_prev_raw[:, hc].astype(F32), kg)
            kwin = jnp.concatenate([k_prv, k_cur], axis=0).astype(BF16)
            vwin = jnp.concatenate([v_prev[:, hc], v_ref[rows, hc]], axis=0)
            qs = []
            for g in range(Q_PER_KV):
                c0 = (h * Q_PER_KV + g) * HEAD_DIM
                qs.append(rms(q_ref[rows, c0:c0 + HEAD_DIM].astype(F32), qg))
            q4 = jnp.concatenate(qs, axis=0).astype(BF16)
            s2 = lax.dot_general(q4, kwin, (((1,), (1,)), ((), ())),
                                 preferred_element_type=F32)
            s = jnp.where(cur4, s2[:, BLK:], s2[:, :BLK])
            slope = slope_ref[h]
            sink = sink_ref[h]
            s = s - slope * dist4
            if has_prev is not None:
                s = jnp.where(jnp.logical_or(cur4, has_prev), s, -jnp.inf)
            m = jnp.maximum(jnp.max(s, axis=-1, keepdims=True), sink)
            p = jnp.exp(s - m)
            denom = jnp.sum(p, axis=-1, keepdims=True) + jnp.exp(sink - m)
            probs = p / denom
            p2 = jnp.concatenate([jnp.where(cur4, 0.0, probs), jnp.where(cur4, probs, 0.0)],
                                 axis=1).astype(BF16)
            o = _dot(p2, vwin)
            for g in range(Q_PER_KV):
                c0 = (h * Q_PER_KV + g) * HEAD_DIM
                o_ref[rows, c0:c0 + HEAD_DIM] = o[g * BLK:(g + 1) * BLK].astype(o_ref.dtype)


def _attention(proj, q_gain, k_gain, sinks, seq_len, blocks_per_step=4):
    n = proj.shape[0]
    tq = blocks_per_step * BLK
    k_col, v_col = ATTN_W // KV_W, ATTN_W // KV_W + 1
    slopes = 2.0 ** (-8.0 * jnp.arange(1, N_Q_HEADS + 1, dtype=F32) / N_Q_HEADS)

    def per_row(vec):
        return jnp.repeat(vec.astype(F32).reshape(N_KV_HEADS, Q_PER_KV), BLK, axis=1)[..., None]

    def prev_map(col):
        return lambda i: (jnp.maximum(i * blocks_per_step - 1, 0), col)

    kern = functools.partial(_attn_kernel, blocks_per_step=blocks_per_step,
                             blocks_per_seq=seq_len // BLK)
    return pl.pallas_call(
        kern,
        grid=(n // tq,),
        in_specs=[pl.BlockSpec((tq, ATTN_W), lambda i: (i, 0)),
                  pl.BlockSpec((tq, KV_W), lambda i: (i, k_col)),
                  pl.BlockSpec((tq, KV_W), lambda i: (i, v_col)),
                  pl.BlockSpec((BLK, KV_W), prev_map(k_col)),
                  pl.BlockSpec((BLK, KV_W), prev_map(v_col)),
                  pl.BlockSpec((1, HEAD_DIM), lambda i: (0, 0)),
                  pl.BlockSpec((1, HEAD_DIM), lambda i: (0, 0)),
                  pl.BlockSpec((N_KV_HEADS, Q_PER_KV * BLK, 1), lambda i: (0, 0, 0)),
                  pl.BlockSpec((N_KV_HEADS, Q_PER_KV * BLK, 1), lambda i: (0, 0, 0))],
        out_specs=pl.BlockSpec((tq, ATTN_W), lambda i: (i, 0)),
        out_shape=jax.ShapeDtypeStruct((n, ATTN_W), BF16),
        compiler_params=_cparams(("parallel",)),
        name="swa_attention",
    )(proj, proj, proj, proj, proj, q_gain.reshape(1, HEAD_DIM).astype(F32),
      k_gain.reshape(1, HEAD_DIM).astype(F32), per_row(slopes), per_row(sinks))


def _sg_kernel(u_ref, v_ref, lng_ref, lnb_ref, w_ref, b_ref, o_ref, *, chunks_per_step):
    row = lax.broadcasted_iota(jnp.int32, (BLK, BLK), 0)
    col = lax.broadcasted_iota(jnp.int32, (BLK, BLK), 1)
    tril = col <= row
    for c in range(chunks_per_step):
        rows = slice(c * BLK, (c + 1) * BLK)
        for g in range(SG_GROUPS):
            gc = slice(g * BLK, (g + 1) * BLK)
            zv = _gelu(v_ref[rows, gc].astype(F32))
            mu = jnp.mean(zv, axis=-1, keepdims=True)
            zc = zv - mu
            var = jnp.mean(zc * zc, axis=-1, keepdims=True)
            zn = zc * lax.rsqrt(var + EPS) * lng_ref[:, gc] + lnb_ref[:, gc]
            w = jnp.where(tril, w_ref[g], 0.0).astype(BF16)
            mixed = _dot(w, zn.astype(BF16)) + b_ref[g]
            zu = _gelu(u_ref[rows, gc].astype(F32))
            o_ref[rows, gc] = (zu * mixed).astype(o_ref.dtype)


def _spatial_gating(proj, ln_g, ln_b, w_s, b_s, chunks_per_step=4):
    n = proj.shape[0]
    tq = chunks_per_step * BLK
    u_col = (ATTN_W + 2 * KV_W) // SG_W
    return pl.pallas_call(
        functools.partial(_sg_kernel, chunks_per_step=chunks_per_step),
        grid=(n // tq,),
        in_specs=[pl.BlockSpec((tq, SG_W), lambda i: (i, u_col)),
                  pl.BlockSpec((tq, SG_W), lambda i: (i, u_col + 1)),
                  pl.BlockSpec((1, SG_W), lambda i: (0, 0)),
                  pl.BlockSpec((1, SG_W), lambda i: (0, 0)),
                  pl.BlockSpec((SG_GROUPS, BLK, BLK), lambda i: (0, 0, 0)),
                  pl.BlockSpec((SG_GROUPS, BLK, 1), lambda i: (0, 0, 0))],
        out_specs=pl.BlockSpec((tq, SG_W), lambda i: (i, 0)),
        out_shape=jax.ShapeDtypeStruct((n, SG_W), BF16),
        compiler_params=_cparams(("parallel",)),
        name="spatial_gating",
    )(proj, proj, ln_g.reshape(1, SG_W).astype(F32), ln_b.reshape(1, SG_W).astype(F32),
      w_s.astype(F32), b_s.astype(F32)[..., None])


def _ssm_operators(a_re, a_im, log_dt, b_re, b_im, c_re, c_im):
    hp = lax.Precision.HIGHEST
    g, p, c, kt = SSM_GROUPS, SSM_STATE, SSM_CH, SSM_KT
    gl = g // kt
    a_re, a_im = a_re.astype(F32), a_im.astype(F32)
    dt = jnp.exp(log_dt.astype(F32))[:, None]

    def apow(k):
        mag = jnp.exp(a_re * dt * k)
        return mag * jnp.cos(a_im * dt * k), mag * jnp.sin(a_im * dt * k)

    ar, ai = apow(1.0)
    den = a_re * a_re + a_im * a_im
    cr = ((ar - 1.0) * a_re + ai * a_im) / den
    ci = (ai * a_re - (ar - 1.0) * a_im) / den
    bbr = cr[..., None] * b_re - ci[..., None] * b_im
    bbi = cr[..., None] * b_im + ci[..., None] * b_re
    eye = jnp.eye(gl, dtype=F32)

    pows = [apow(float(k)) for k in range(SSM_SUB + 1)]
    pr = jnp.stack([x[0] for x in pows])
    pi = jnp.stack([x[1] for x in pows])

    car = c_re[None] * pr[:, :, None, :] - c_im[None] * pi[:, :, None, :]
    cai = c_re[None] * pi[:, :, None, :] + c_im[None] * pr[:, :, None, :]

    kk = (jnp.einsum("lgcp,gpd->lgcd", car[:SSM_SUB], bbr, precision=hp)
          - jnp.einsum("lgcp,gpd->lgcd", cai[:SSM_SUB], bbi, precision=hp))
    kk = kk.reshape(SSM_SUB, kt, gl, c, c)
    bd = jnp.einsum("lkgcd,gh->lkgdhc", kk, eye).reshape(SSM_SUB, kt, gl * c, gl * c)

    rev_r, rev_i = pr[SSM_SUB - 1::-1][:SSM_SUB], pi[SSM_SUB - 1::-1][:SSM_SUB]
    wr = rev_r[..., None] * bbr[None] - rev_i[..., None] * bbi[None]
    wi = rev_r[..., None] * bbi[None] + rev_i[..., None] * bbr[None]

    def state_cols(x):
        x = x.reshape(SSM_SUB, kt, gl, p, c)
        return jnp.einsum("skgpc,gh->skgchp", x, eye).reshape(SSM_SUB, kt, gl * c, gl * p)

    w2 = jnp.concatenate([state_cols(wr), state_cols(wi)], axis=-1)

    def state_rows(x):
        x = x.reshape(SSM_SUB, kt, gl, c, p)
        return jnp.einsum("tkgcp,gh->tkgphc", x, eye).reshape(SSM_SUB, kt, gl * p, gl * c)

    cre = state_rows(car[1:])
    cim = state_rows(-cai[1:])
    a8 = jnp.stack([pr[SSM_SUB].reshape(-1), pi[SSM_SUB].reshape(-1)])
    return bd.astype(BF16), w2.astype(BF16), cre.astype(BF16), cim.astype(BF16), a8


def _ssm_kernel(x_ref, bd_ref, w2_ref, cre_ref, cim_ref, a8_ref, d_ref, wg_ref, bg_ref, o_ref,
                xf_ref, sloc_ref, st_ref, of_ref, carry_ref, *, tiles_per_seq):
    i = pl.program_id(0)
    tm = x_ref.shape[0]
    nj = tm // SSM_SUB
    ns = SSM_NSTATE
    gw = ns // SSM_KT

    @pl.when(i % tiles_per_seq == 0)
    def _():
        carry_ref[...] = jnp.zeros_like(carry_ref)

    for kt in range(SSM_KT):
        xf_ref[kt] = x_ref[:, kt * BLK:(kt + 1) * BLK].astype(F32)
    xs = [[xf_ref[kt, pl.ds(s, nj, stride=SSM_SUB), :] for kt in range(SSM_KT)]
          for s in range(SSM_SUB)]
    xb = [[x.astype(BF16) for x in row] for row in xs]

    for kt in range(SSM_KT):
        acc = _dot(xb[0][kt], w2_ref[0, kt])
        for s in range(1, SSM_SUB):
            acc = acc + _dot(xb[s][kt], w2_ref[s, kt])
        sloc_ref[:, kt * gw:(kt + 1) * gw] = acc[:, :gw]
        sloc_ref[:, ns + kt * gw:ns + (kt + 1) * gw] = acc[:, gw:]

    a8r = a8_ref[0:1, :]
    a8i = a8_ref[1:2, :]

    def step(j, st):
        sr, si = st
        st_ref[pl.ds(j, 1), :ns] = sr
        st_ref[pl.ds(j, 1), ns:] = si
        lr = sloc_ref[pl.ds(j, 1), :ns]
        li = sloc_ref[pl.ds(j, 1), ns:]
        return (a8r * sr - a8i * si + lr, a8r * si + a8i * sr + li)

    sr, si = lax.fori_loop(0, nj, step, (carry_ref[0:1, :], carry_ref[1:2, :]))
    carry_ref[0:1, :] = sr
    carry_ref[1:2, :] = si

    stb = st_ref[...].astype(BF16)
    dvec = d_ref[...]
    wg = wg_ref[...]
    bg = bg_ref[...]
    for t in range(SSM_SUB):
        cols = []
        for kt in range(SSM_KT):
            acc = _dot(stb[:, kt * gw:(kt + 1) * gw], cre_ref[t, kt])
            acc = acc + _dot(stb[:, ns + kt * gw:ns + (kt + 1) * gw], cim_ref[t, kt])
            for s in range(t + 1):
                acc = acc + _dot(xb[s][kt], bd_ref[t - s, kt])
            cols.append(acc)
        y = jnp.concatenate(cols, axis=1) + dvec * jnp.concatenate(xs[t], axis=1)
        y = _gelu(y)
        out = y * _sigmoid(_dot(y.astype(BF16), wg) + bg)
        for kt in range(SSM_KT):
            of_ref[kt, pl.ds(t, nj, stride=SSM_SUB), :] = out[:, kt * BLK:(kt + 1) * BLK]
    for kt in range(SSM_KT):
        o_ref[:, kt * BLK:(kt + 1) * BLK] = of_ref[kt].astype(o_ref.dtype)


def _ssm(proj, ops, d_skip, w_glu, b_glu, seq_len, tm=1024):
    n = proj.shape[0]
    bd, w2, cre, cim, a8 = ops
    s_col = (MAIN_W - SSM_W) // SSM_W
    nj = tm // SSM_SUB
    const = lambda nd: (lambda i: (0,) * nd)
    return pl.pallas_call(
        functools.partial(_ssm_kernel, tiles_per_seq=seq_len // tm),
        grid=(n // tm,),
        in_specs=[pl.BlockSpec((tm, SSM_W), lambda i: (i, s_col)),
                  pl.BlockSpec(bd.shape, const(4)),
                  pl.BlockSpec(w2.shape, const(4)),
                  pl.BlockSpec(cre.shape, const(4)),
                  pl.BlockSpec(cim.shape, const(4)),
                  pl.BlockSpec(a8.shape, const(2)),
                  pl.BlockSpec((1, SSM_W), const(2)),
                  pl.BlockSpec((SSM_W, SSM_W), const(2)),
                  pl.BlockSpec((1, SSM_W), const(2))],
        out_specs=pl.BlockSpec((tm, SSM_W), lambda i: (i, 0)),
        out_shape=jax.ShapeDtypeStruct((n, SSM_W), BF16),
        scratch_shapes=[pltpu.VMEM((SSM_KT, tm, BLK), F32),
                        pltpu.VMEM((nj, 2 * SSM_NSTATE), F32),
                        pltpu.VMEM((nj, 2 * SSM_NSTATE), F32),
                        pltpu.VMEM((SSM_KT, tm, BLK), F32),
                        pltpu.VMEM((2, SSM_NSTATE), F32)],
        compiler_params=_cparams(("arbitrary",)),
        name="s5_ssm_glu",
    )(proj, bd, w2, cre, cim, a8, d_skip.reshape(1, SSM_W).astype(F32), w_glu,
      b_glu.reshape(1, SSM_W).astype(F32))


def _merge_kernel(x_ref, g0_ref, g1_ref, g2_ref, ya_ref, ys_ref, ym_ref, bg_ref,
                  wa_ref, ws_ref, wm_ref, wo_ref, n2_ref, xo_ref, h_ref):
    d = D_MODEL
    merged = _sigmoid(g0_ref[...].astype(F32) + bg_ref[:, 0:d]) * _dot(ya_ref[...], wa_ref[...])
    merged = merged + _sigmoid(g1_ref[...].astype(F32) + bg_ref[:, d:2 * d]) * _dot(ys_ref[...], ws_ref[...])
    merged = merged + _sigmoid(g2_ref[...].astype(F32) + bg_ref[:, 2 * d:]) * _dot(ym_ref[...], wm_ref[...])
    xn = x_ref[...] + _dot(merged.astype(BF16), wo_ref[...])
    xo_ref[...] = xn
    ms = jnp.mean(xn * xn, axis=-1, keepdims=True)
    h_ref[...] = (xn * lax.rsqrt(ms + EPS) * n2_ref[...]).astype(h_ref.dtype)


def _merge(x, gates, y_attn, y_sg, y_ssm, b_gate, wa, ws, wm, wo, norm2_g, tm=256):
    n, d = x.shape
    row = lambda w: pl.BlockSpec((tm, w), lambda i: (i, 0))
    res = lambda shape: pl.BlockSpec(shape, lambda i: (0, 0), pipeline_mode=pl.Buffered(1))
    return pl.pallas_call(
        _merge_kernel,
        grid=(n // tm,),
        in_specs=[row(d),
                  pl.BlockSpec((tm, d), lambda i: (i, 0)),
                  pl.BlockSpec((tm, d), lambda i: (i, 1)),
                  pl.BlockSpec((tm, d), lambda i: (i, 2)),
                  row(ATTN_W), row(SG_W), row(SSM_W),
                  res((1, GATE_W)), res(wa.shape), res(ws.shape), res(wm.shape), res(wo.shape),
                  res((1, d))],
        out_specs=[row(d), row(d)],
        out_shape=[jax.ShapeDtypeStruct((n, d), F32), jax.ShapeDtypeStruct((n, d), BF16)],
        compiler_params=_cparams(("parallel",)),
        name="merge_outproj",
    )(x, gates, gates, gates, y_attn, y_sg, y_ssm, b_gate.reshape(1, GATE_W).astype(F32),
      wa, ws, wm, wo, norm2_g.reshape(1, d).astype(F32))


FFN_HALO = 16


def _ffn_up_kernel(h_ref, hp_ref, wg_ref, wv_ref, cwg_ref, cwv_ref, cbg_ref, cbv_ref, o_ref, lhs_ref,
                   *, tiles_per_seq):
    i = pl.program_id(0)
    j = pl.program_id(1)
    tm = h_ref.shape[0]

    @pl.when(j == 0)
    def _():
        keep = (i % tiles_per_seq != 0).astype(hp_ref.dtype)
        lhs_ref[0:FFN_HALO, :] = hp_ref[...] * keep
        lhs_ref[FFN_HALO:, :] = h_ref[...]

    lhs = lhs_ref[...]

    def conv(w_ref, cw_ref, cb_ref):
        e = _dot(lhs, w_ref[...])
        e1 = pltpu.roll(e, 1, 0)
        e2 = pltpu.roll(e, 2, 0)
        return (cb_ref[...] + cw_ref[0:1, :] * e[FFN_HALO:] + cw_ref[1:2, :] * e1[FFN_HALO:]
                + cw_ref[2:3, :] * e2[FFN_HALO:])

    gate = conv(wg_ref, cwg_ref, cbg_ref)
    val = conv(wv_ref, cwv_ref, cbv_ref)
    o_ref[...] = (_gelu(gate) * val).astype(o_ref.dtype)


def _ffn_up(h, w_up, conv_w, conv_b, seq_len, tm=512, tn=512):
    n, d = h.shape
    nj = D_FF // tn
    return pl.pallas_call(
        functools.partial(_ffn_up_kernel, tiles_per_seq=seq_len // tm),
        grid=(n // tm, nj),
        in_specs=[pl.BlockSpec((tm, d), lambda i, j: (i, 0)),
                  pl.BlockSpec((FFN_HALO, d),
                               lambda i, j: (jnp.maximum(i * (tm // FFN_HALO) - 1, 0), 0)),
                  pl.BlockSpec((d, tn), lambda i, j: (0, j)),
                  pl.BlockSpec((d, tn), lambda i, j: (0, j + nj)),
                  pl.BlockSpec((3, tn), lambda i, j: (0, j)),
                  pl.BlockSpec((3, tn), lambda i, j: (0, j + nj)),
                  pl.BlockSpec((1, tn), lambda i, j: (0, j)),
                  pl.BlockSpec((1, tn), lambda i, j: (0, j + nj))],
        out_specs=pl.BlockSpec((tm, tn), lambda i, j: (i, j)),
        out_shape=jax.ShapeDtypeStruct((n, D_FF), BF16),
        scratch_shapes=[pltpu.VMEM((tm + FFN_HALO, d), BF16)],
        compiler_params=_cparams(("parallel", "arbitrary")),
        name="ffn_up_conv",
    )(h, h, w_up, w_up, conv_w, conv_w, conv_b.reshape(1, -1), conv_b.reshape(1, -1))


def _ffn_down_kernel(a_ref, w_ref, x_ref, g_ref, xo_ref, h_ref, xs_ref):
    j = pl.program_id(1)
    nj = pl.num_programs(1)
    tn = w_ref.shape[1]
    xn = x_ref[...] + _dot(a_ref[...], w_ref[...])
    xo_ref[...] = xn
    xs_ref[j] = xn

    @pl.when(j == nj - 1)
    def _():
        ssq = jnp.sum(xs_ref[0] * xs_ref[0], axis=-1, keepdims=True)
        for c in range(1, xs_ref.shape[0]):
            ssq = ssq + jnp.sum(xs_ref[c] * xs_ref[c], axis=-1, keepdims=True)
        inv = lax.rsqrt(ssq * (1.0 / D_MODEL) + EPS)
        for c in range(xs_ref.shape[0]):
            h_ref[:, c * tn:(c + 1) * tn] = (xs_ref[c] * inv * g_ref[:, c * tn:(c + 1) * tn]).astype(h_ref.dtype)


def _ffn_down(act, w_down, x, next_g, tm=512, tn=512):
    n, d = x.shape
    k = act.shape[1]
    nj = d // tn
    return pl.pallas_call(
        _ffn_down_kernel,
        grid=(n // tm, nj),
        in_specs=[pl.BlockSpec((tm, k), lambda i, j: (i, 0)),
                  pl.BlockSpec((k, tn), lambda i, j: (0, j)),
                  pl.BlockSpec((tm, tn), lambda i, j: (i, j)),
                  pl.BlockSpec((1, d), lambda i, j: (0, 0))],
        out_specs=[pl.BlockSpec((tm, tn), lambda i, j: (i, j)),
                   pl.BlockSpec((tm, d), lambda i, j: (i, 0))],
        out_shape=[jax.ShapeDtypeStruct((n, d), F32), jax.ShapeDtypeStruct((n, d), BF16)],
        scratch_shapes=[pltpu.VMEM((nj, tm, tn), F32)],
        compiler_params=_cparams(("parallel", "arbitrary")),
        name="ffn_down",
    )(act, w_down, x, next_g.reshape(1, d).astype(F32))


def kernel(x, norm1_g, w_in, b_gate, q_norm_g, k_norm_g, attn_sinks, sg_ln_g, sg_ln_b, sg_w, sg_b, ssm_a_re, ssm_a_im, ssm_log_dt, ssm_b_re, ssm_b_im, ssm_c_re, ssm_c_im, ssm_d, ssm_w_glu, ssm_b_glu, w_proj_attn, w_proj_sg, w_proj_ssm, w_out, norm2_g, ffn_w_up, ffn_conv_w, ffn_conv_b, ffn_w_down):
    b, seq_len, d = x.shape
    depth = w_in.shape[0]
    xr = x.reshape(b * seq_len, d)
    h = _rmsnorm(xr, norm1_g[0].astype(F32))
    for l in range(depth):
        w_main = w_in[l, :, :MAIN_W].astype(BF16)
        w_gate = w_in[l, :, MAIN_W:].astype(BF16)
        proj = _matmul(h, w_main, name="in_proj_main")
        gates = _matmul(h, w_gate, name="in_proj_gates")
        y_attn = _attention(proj, q_norm_g[l], k_norm_g[l], attn_sinks[l], seq_len)
        y_sg = _spatial_gating(proj, sg_ln_g[l], sg_ln_b[l], sg_w[l], sg_b[l])
        ops = _ssm_operators(ssm_a_re[l], ssm_a_im[l], ssm_log_dt[l], ssm_b_re[l], ssm_b_im[l],
                             ssm_c_re[l], ssm_c_im[l])
        y_ssm = _ssm(proj, ops, ssm_d[l], ssm_w_glu[l].astype(BF16), ssm_b_glu[l], seq_len)
        xr, h2 = _merge(xr, gates, y_attn, y_sg, y_ssm, b_gate[l],
                        w_proj_attn[l].astype(BF16), w_proj_sg[l].astype(BF16),
                        w_proj_ssm[l].astype(BF16), w_out[l].astype(BF16), norm2_g[l])
        act = _ffn_up(h2, ffn_w_up[l].astype(BF16), ffn_conv_w[l].astype(F32),
                      ffn_conv_b[l].astype(F32), seq_len)
        next_g = norm1_g[l + 1] if l + 1 < depth else norm1_g[l]
        xr, h = _ffn_down(act, ffn_w_down[l].astype(BF16), xr, next_g)
    return xr.reshape(b, seq_len, d)
```

```python
import functools
import math

import jax
import jax.numpy as jnp
from jax import lax
from jax.experimental import pallas as pl
from jax.experimental.pallas import tpu as pltpu

F32 = jnp.float32
BF16 = jnp.bfloat16

D_MODEL = 2048
N_Q_HEADS = 8
N_KV_HEADS = 2
Q_PER_KV = N_Q_HEADS // N_KV_HEADS
HEAD_DIM = 128
BLK = 128
ATTN_W = N_Q_HEADS * HEAD_DIM
KV_W = N_KV_HEADS * HEAD_DIM
SG_GROUPS = 4
SG_W = 512
SSM_GROUPS = 32
SSM_CH = 16
SSM_W = 512
SSM_STATE = 64
SSM_SUB = 8
SSM_KT = 4
SSM_NSTATE = SSM_GROUPS * SSM_STATE
MAIN_W = ATTN_W + 2 * KV_W + 2 * SG_W + SSM_W
GATE_W = 3 * D_MODEL
IN_W = MAIN_W + GATE_W
D_FF = 5632
F32_SUBLANES = 8
EPS = 1e-6

VMEM_LIMIT = 56 * 1024 * 1024


def _cparams(sem):
    return pltpu.CompilerParams(dimension_semantics=sem, vmem_limit_bytes=VMEM_LIMIT)


def _gelu(x):
    c = math.sqrt(2.0 / math.pi)
    return 0.5 * x * (1.0 + jnp.tanh(c * (x + 0.044715 * (x * x * x))))


def _sigmoid(x):
    return 1.0 / (1.0 + jnp.exp(-x))


def _dot(a, b):
    return jnp.dot(a, b, preferred_element_type=F32)


def _row_inv_rms(x):
    return lax.rsqrt(jnp.mean(x * x, axis=-1, keepdims=True) + EPS)


def _rmsnorm_kernel(x_ref, g_ref, o_ref, inv_ref):
    x = x_ref[...]
    o_ref[...] = (x * g_ref[...]).astype(o_ref.dtype)
    inv_ref[...] = _row_inv_rms(x)


def _rmsnorm(x, g, layer, tm=512):
    n, d = x.shape
    return pl.pallas_call(
        _rmsnorm_kernel,
        grid=(n // tm,),
        in_specs=[pl.BlockSpec((tm, d), lambda i: (i, 0)),
                  pl.BlockSpec((None, 1, d), lambda i: (layer, 0, 0))],
        out_specs=[pl.BlockSpec((tm, d), lambda i: (i, 0)), pl.BlockSpec((tm, 1), lambda i: (i, 0))],
        out_shape=[jax.ShapeDtypeStruct((n, d), BF16), jax.ShapeDtypeStruct((n, 1), F32)],
        compiler_params=_cparams(("parallel",)),
        name="rmsnorm",
    )(x, g)


def _in_proj_kernel(h_ref, inv_ref, w_ref, o_ref, wb_ref):
    @pl.when(pl.program_id(1) == 0)
    def _():
        wb_ref[...] = w_ref[...].astype(BF16)

    o_ref[...] = (_dot(h_ref[...], wb_ref[...]) * inv_ref[...]).astype(o_ref.dtype)


def _in_proj(h, inv, w_in, layer, tm=1024, tn=1024):
    n, d = h.shape
    width = w_in.shape[2]
    return pl.pallas_call(
        _in_proj_kernel,
        grid=(width // tn, n // tm),
        in_specs=[pl.BlockSpec((tm, d), lambda j, i: (i, 0)),
                  pl.BlockSpec((tm, 1), lambda j, i: (i, 0)),
                  pl.BlockSpec((None, d, tn), lambda j, i: (layer, 0, j))],
        out_specs=pl.BlockSpec((tm, tn), lambda j, i: (i, j)),
        out_shape=jax.ShapeDtypeStruct((n, width), BF16),
        scratch_shapes=[pltpu.VMEM((d, tn), BF16)],
        compiler_params=_cparams(("parallel", "arbitrary")),
        name="in_proj",
    )(h, inv, w_in)


def _attn_kernel(q_ref, k_ref, v_ref, kp_ref, vp_ref, qg_ref, kg_ref, bias_ref, sink_ref, o_ref,
                 *, blocks_per_step, blocks_per_seq):
    i = pl.program_id(0)
    qg = qg_ref[...] * (HEAD_DIM ** -0.5)
    kg = kg_ref[...]
    row = lax.broadcasted_iota(jnp.int32, (BLK, BLK), 0)
    col = lax.broadcasted_iota(jnp.int32, (BLK, BLK), 1)
    cur4 = jnp.concatenate([col <= row] * Q_PER_KV, axis=0)
    ones = jnp.ones((BLK, BLK), BF16)
    ones_win = jnp.ones((2 * BLK, BLK), BF16)

    def rms(x, g):
        ssq = _dot((x * x).astype(BF16), ones)
        return x * lax.rsqrt(ssq * (1.0 / HEAD_DIM) + EPS) * g

    k_prev = None
    for r in range(blocks_per_step):
        rows = slice(r * BLK, (r + 1) * BLK)
        if r == 0:
            has_prev = (i * blocks_per_step) % blocks_per_seq != 0
            k_prev = [rms(kp_ref[:, h * HEAD_DIM:(h + 1) * HEAD_DIM].astype(F32), kg)
                      for h in range(N_KV_HEADS)]
            v_prev = vp_ref[...]
        else:
            has_prev = None
            v_prev = v_ref[(r - 1) * BLK:r * BLK, :]
        k_cur = []
        for h in range(N_KV_HEADS):
            hc = slice(h * HEAD_DIM, (h + 1) * HEAD_DIM)
            k_cur.append(rms(k_ref[rows, hc].astype(F32), kg))
            kwin = jnp.concatenate([k_prev[h], k_cur[h]], axis=0).astype(BF16)
            vwin = jnp.concatenate([v_prev[:, hc], v_ref[rows, hc]], axis=0)
            vext = jnp.concatenate([vwin, ones_win], axis=1)
            q4 = jnp.concatenate(
                [q_ref[rows, (h * Q_PER_KV + g) * HEAD_DIM:(h * Q_PER_KV + g + 1) * HEAD_DIM].astype(F32)
                 for g in range(Q_PER_KV)], axis=0)
            q4 = rms(q4, qg).astype(BF16)
            s2 = lax.dot_general(q4, kwin, (((1,), (1,)), ((), ())),
                                 preferred_element_type=F32)
            s = jnp.where(cur4, s2[:, BLK:], s2[:, :BLK]) - bias_ref[h]
            if has_prev is not None:
                s = jnp.where(jnp.logical_or(cur4, has_prev), s, -jnp.inf)
            sink = sink_ref[h]
            m = jnp.maximum(jnp.max(s, axis=-1, keepdims=True), sink)
            p = jnp.exp(s - m)
            p2 = jnp.concatenate([jnp.where(cur4, 0.0, p), jnp.where(cur4, p, 0.0)],
                                 axis=1).astype(BF16)
            ov = _dot(p2, vext)
            o = ov[:, :BLK] / (ov[:, BLK:] + jnp.exp(sink - m))
            for g in range(Q_PER_KV):
                c0 = (h * Q_PER_KV + g) * HEAD_DIM
                o_ref[rows, c0:c0 + HEAD_DIM] = o[g * BLK:(g + 1) * BLK].astype(o_ref.dtype)
        k_prev = k_cur


def _attention(proj, q_gain, k_gain, sinks, seq_len, blocks_per_step=4):
    n = proj.shape[0]
    tq = blocks_per_step * BLK
    k_col, v_col = ATTN_W // KV_W, ATTN_W // KV_W + 1
    slopes = 2.0 ** (-8.0 * jnp.arange(1, N_Q_HEADS + 1, dtype=F32) / N_Q_HEADS)
    ri = jnp.arange(BLK)[:, None]
    ci = jnp.arange(BLK)[None, :]
    dist = jnp.where(ci <= ri, ri - ci, ri - ci + BLK).astype(F32)
    bias = (slopes[:, None, None] * dist).reshape(N_KV_HEADS, Q_PER_KV * BLK, BLK)
    sink_tile = jnp.broadcast_to(sinks.astype(F32)[:, None, None], (N_Q_HEADS, BLK, BLK))
    sink_tile = sink_tile.reshape(N_KV_HEADS, Q_PER_KV * BLK, BLK)

    def prev_map(col):
        return lambda i: (jnp.maximum(i * blocks_per_step - 1, 0), col)

    head_tile = pl.BlockSpec((N_KV_HEADS, Q_PER_KV * BLK, BLK), lambda i: (0, 0, 0))
    kern = functools.partial(_attn_kernel, blocks_per_step=blocks_per_step,
                             blocks_per_seq=seq_len // BLK)
    return pl.pallas_call(
        kern,
        grid=(n // tq,),
        in_specs=[pl.BlockSpec((tq, ATTN_W), lambda i: (i, 0)),
                  pl.BlockSpec((tq, KV_W), lambda i: (i, k_col)),
                  pl.BlockSpec((tq, KV_W), lambda i: (i, v_col)),
                  pl.BlockSpec((BLK, KV_W), prev_map(k_col)),
                  pl.BlockSpec((BLK, KV_W), prev_map(v_col)),
                  pl.BlockSpec((1, HEAD_DIM), lambda i: (0, 0)),
                  pl.BlockSpec((1, HEAD_DIM), lambda i: (0, 0)),
                  head_tile, head_tile],
        out_specs=pl.BlockSpec((tq, ATTN_W), lambda i: (i, 0)),
        out_shape=jax.ShapeDtypeStruct((n, ATTN_W), BF16),
        compiler_params=_cparams(("parallel",)),
        name="swa_attention",
    )(proj, proj, proj, proj, proj, q_gain.reshape(1, HEAD_DIM).astype(F32),
      k_gain.reshape(1, HEAD_DIM).astype(F32), bias, sink_tile)


def _sg_kernel(u_ref, v_ref, lng_ref, lnb_ref, w_ref, b_ref, o_ref, *, chunks_per_step):
    row = lax.broadcasted_iota(jnp.int32, (BLK, BLK), 0)
    col = lax.broadcasted_iota(jnp.int32, (BLK, BLK), 1)
    tril = col <= row
    for c in range(chunks_per_step):
        rows = slice(c * BLK, (c + 1) * BLK)
        for g in range(SG_GROUPS):
            gc = slice(g * BLK, (g + 1) * BLK)
            zv = _gelu(v_ref[rows, gc].astype(F32))
            mu = jnp.mean(zv, axis=-1, keepdims=True)
            zc = zv - mu
            var = jnp.mean(zc * zc, axis=-1, keepdims=True)
            zn = zc * lax.rsqrt(var + EPS) * lng_ref[:, gc] + lnb_ref[:, gc]
            w = jnp.where(tril, w_ref[g], 0.0).astype(BF16)
            mixed = _dot(w, zn.astype(BF16)) + b_ref[g]
            zu = _gelu(u_ref[rows, gc].astype(F32))
            o_ref[rows, gc] = (zu * mixed).astype(o_ref.dtype)


def _spatial_gating(proj, ln_g, ln_b, w_s, b_s, chunks_per_step=4):
    n = proj.shape[0]
    tq = chunks_per_step * BLK
    u_col = (ATTN_W + 2 * KV_W) // SG_W
    return pl.pallas_call(
        functools.partial(_sg_kernel, chunks_per_step=chunks_per_step),
        grid=(n // tq,),
        in_specs=[pl.BlockSpec((tq, SG_W), lambda i: (i, u_col)),
                  pl.BlockSpec((tq, SG_W), lambda i: (i, u_col + 1)),
                  pl.BlockSpec((1, SG_W), lambda i: (0, 0)),
                  pl.BlockSpec((1, SG_W), lambda i: (0, 0)),
                  pl.BlockSpec((SG_GROUPS, BLK, BLK), lambda i: (0, 0, 0)),
                  pl.BlockSpec((SG_GROUPS, BLK, 1), lambda i: (0, 0, 0))],
        out_specs=pl.BlockSpec((tq, SG_W), lambda i: (i, 0)),
        out_shape=jax.ShapeDtypeStruct((n, SG_W), BF16),
        compiler_params=_cparams(("parallel",)),
        name="spatial_gating",
    )(proj, proj, ln_g.reshape(1, SG_W).astype(F32), ln_b.reshape(1, SG_W).astype(F32),
      w_s.astype(F32), b_s.astype(F32)[..., None])


def _group_mask(rows_per_group, cols_per_group):
    n = SSM_GROUPS // SSM_KT
    r = jnp.arange(n * rows_per_group)[:, None] // rows_per_group
    c = jnp.arange(n * cols_per_group)[None, :] // cols_per_group
    return (r == c).astype(F32)


def _ssm_operators(a_re, a_im, log_dt, b_re, b_im, c_re, c_im):
    hp = lax.Precision.HIGHEST
    p, c, kt = SSM_STATE, SSM_CH, SSM_KT
    gl = SSM_GROUPS // kt
    nl = a_re.shape[0]
    a_re, a_im = a_re.astype(F32)[:, None], a_im.astype(F32)[:, None]
    dt = jnp.exp(log_dt.astype(F32))[:, None, :, None]
    k = jnp.arange(SSM_SUB + 1, dtype=F32)[None, :, None, None]
    mag = jnp.exp(a_re * dt * k)
    pr, pi = mag * jnp.cos(a_im * dt * k), mag * jnp.sin(a_im * dt * k)
    ar, ai, lr, li = pr[:, 1], pi[:, 1], a_re[:, 0], a_im[:, 0]
    den = lr * lr + li * li
    cr = ((ar - 1.0) * lr + ai * li) / den
    ci = (ai * lr - (ar - 1.0) * li) / den
    bbr = cr[..., None] * b_re - ci[..., None] * b_im
    bbi = cr[..., None] * b_im + ci[..., None] * b_re
    bbr_t, bbi_t = jnp.swapaxes(bbr, -1, -2), jnp.swapaxes(bbi, -1, -2)
    car = c_re[:, None] * pr[:, :, :, None, :] - c_im[:, None] * pi[:, :, :, None, :]
    cai = c_re[:, None] * pi[:, :, :, None, :] + c_im[:, None] * pr[:, :, :, None, :]

    def blockdiag(x, rows, cols):
        x = x.reshape(nl, SSM_SUB, kt, gl * rows, cols)
        return jnp.tile(x, (1, 1, 1, 1, gl)) * _group_mask(rows, cols)

    kk = (jnp.einsum("xgdp,xlgcp->xlgdc", bbr_t, car[:, :SSM_SUB], precision=hp)
          - jnp.einsum("xgdp,xlgcp->xlgdc", bbi_t, cai[:, :SSM_SUB], precision=hp))
    bd = blockdiag(kk, c, c)
    rr, ri = pr[:, SSM_SUB - 1::-1][:, :, :, None, :], pi[:, SSM_SUB - 1::-1][:, :, :, None, :]
    wr = rr * bbr_t[:, None] - ri * bbi_t[:, None]
    wi = rr * bbi_t[:, None] + ri * bbr_t[:, None]
    w2 = jnp.concatenate([blockdiag(wr, c, p), blockdiag(wi, c, p)], axis=-1)
    cre = blockdiag(jnp.swapaxes(car[:, 1:], -1, -2), p, c)
    cim = blockdiag(-jnp.swapaxes(cai[:, 1:], -1, -2), p, c)
    a8 = jnp.stack([pr[:, SSM_SUB].reshape(nl, -1), pi[:, SSM_SUB].reshape(nl, -1)], axis=1)
    return bd.astype(BF16), w2.astype(BF16), cre.astype(BF16), cim.astype(BF16), a8


def _ssm_kernel(x_ref, bd_ref, w2_ref, cre_ref, cim_ref, a8_ref, d_ref, wg_ref, bg_ref, o_ref,
                xf_ref, sloc_ref, st_ref, of_ref, carry_ref, *, tiles_per_seq):
    i = pl.program_id(0)
    tm = x_ref.shape[0]
    nj = tm // SSM_SUB
    ns = SSM_NSTATE
    gw = ns // SSM_KT

    @pl.when(i % tiles_per_seq == 0)
    def _():
        carry_ref[...] = jnp.zeros_like(carry_ref)

    for kt in range(SSM_KT):
        xf_ref[kt] = x_ref[:, kt * BLK:(kt + 1) * BLK].astype(F32)
    xs = [[xf_ref[kt, pl.ds(s, nj, stride=SSM_SUB), :] for kt in range(SSM_KT)]
          for s in range(SSM_SUB)]
    xb = [[x.astype(BF16) for x in row] for row in xs]

    for kt in range(SSM_KT):
        acc = _dot(xb[0][kt], w2_ref[0, kt])
        for s in range(1, SSM_SUB):
            acc = acc + _dot(xb[s][kt], w2_ref[s, kt])
        sloc_ref[:, kt * gw:(kt + 1) * gw] = acc[:, :gw]
        sloc_ref[:, ns + kt * gw:ns + (kt + 1) * gw] = acc[:, gw:]

    a8r = a8_ref[0:1, :]
    a8i = a8_ref[1:2, :]

    def step(j, st):
        sr, si = st
        st_ref[pl.ds(j, 1), :ns] = sr
        st_ref[pl.ds(j, 1), ns:] = si
        lr = sloc_ref[pl.ds(j, 1), :ns]
        li = sloc_ref[pl.ds(j, 1), ns:]
        return (a8r * sr - a8i * si + lr, a8r * si + a8i * sr + li)

    sr, si = lax.fori_loop(0, nj, step, (carry_ref[0:1, :], carry_ref[1:2, :]))
    carry_ref[0:1, :] = sr
    carry_ref[1:2, :] = si

    stb = st_ref[...].astype(BF16)
    dvec = d_ref[...]
    wg = wg_ref[...]
    bg = bg_ref[...]
    for t in range(SSM_SUB):
        cols = []
        for kt in range(SSM_KT):
            acc = _dot(stb[:, kt * gw:(kt + 1) * gw], cre_ref[t, kt])
            acc = acc + _dot(stb[:, ns + kt * gw:ns + (kt + 1) * gw], cim_ref[t, kt])
            for s in range(t + 1):
                acc = acc + _dot(xb[s][kt], bd_ref[t - s, kt])
            cols.append(acc)
        y = jnp.concatenate(cols, axis=1) + dvec * jnp.concatenate(xs[t], axis=1)
        y = _gelu(y)
        out = y * _sigmoid(_dot(y.astype(BF16), wg) + bg)
        for kt in range(SSM_KT):
            of_ref[kt, pl.ds(t, nj, stride=SSM_SUB), :] = out[:, kt * BLK:(kt + 1) * BLK]
    for kt in range(SSM_KT):
        o_ref[:, kt * BLK:(kt + 1) * BLK] = of_ref[kt].astype(o_ref.dtype)


def _ssm(proj, ops, d_skip, w_glu, b_glu, layer, seq_len, tm=1024):
    n = proj.shape[0]
    bd, w2, cre, cim, a8 = ops
    s_col = (MAIN_W - SSM_W) // SSM_W
    nj = tm // SSM_SUB

    def layer_spec(arr):
        nd = arr.ndim - 1
        return pl.BlockSpec((None,) + arr.shape[1:], lambda i: (layer,) + (0,) * nd)

    return pl.pallas_call(
        functools.partial(_ssm_kernel, tiles_per_seq=seq_len // tm),
        grid=(n // tm,),
        in_specs=[pl.BlockSpec((tm, SSM_W), lambda i: (i, s_col)),
                  layer_spec(bd), layer_spec(w2), layer_spec(cre), layer_spec(cim), layer_spec(a8),
                  layer_spec(d_skip), layer_spec(w_glu), layer_spec(b_glu)],
        out_specs=pl.BlockSpec((tm, SSM_W), lambda i: (i, 0)),
        out_shape=jax.ShapeDtypeStruct((n, SSM_W), BF16),
        scratch_shapes=[pltpu.VMEM((SSM_KT, tm, BLK), F32),
                        pltpu.VMEM((nj, 2 * SSM_NSTATE), F32),
                        pltpu.VMEM((nj, 2 * SSM_NSTATE), F32),
                        pltpu.VMEM((SSM_KT, tm, BLK), F32),
                        pltpu.VMEM((2, SSM_NSTATE), F32)],
        compiler_params=_cparams(("arbitrary",)),
        name="s5_ssm_glu",
    )(proj, bd, w2, cre, cim, a8, d_skip, w_glu, b_glu)


def _merge_kernel(x_ref, g0a_ref, g0b_ref, g1a_ref, g1b_ref, g2a_ref, g2b_ref, ya_ref, ys_ref, ym_ref,
                  bg_ref, wa_ref, ws_ref, wm_ref, wo_ref, n2_ref, xo_ref, h_ref):
    d = D_MODEL

    def gate(lo_ref, hi_ref, b):
        logits = jnp.concatenate([lo_ref[...], hi_ref[...]], axis=1).astype(F32)
        return _sigmoid(logits + bg_ref[:, b * d:(b + 1) * d])

    merged = gate(g0a_ref, g0b_ref, 0) * _dot(ya_ref[...], wa_ref[...])
    merged = merged + gate(g1a_ref, g1b_ref, 1) * _dot(ys_ref[...], ws_ref[...])
    merged = merged + gate(g2a_ref, g2b_ref, 2) * _dot(ym_ref[...], wm_ref[...])
    xn = x_ref[...] + _dot(merged.astype(BF16), wo_ref[...])
    xo_ref[...] = xn
    h_ref[...] = (xn * _row_inv_rms(xn) * n2_ref[...]).astype(h_ref.dtype)


def _merge(x, proj, y_attn, y_sg, y_ssm, b_gate, wa, ws, wm, wo, norm2_g, layer, tm=256):
    n, d = x.shape
    half = d // 2
    g_col = MAIN_W // half
    row = lambda w: pl.BlockSpec((tm, w), lambda i: (i, 0))
    gate = lambda c: pl.BlockSpec((tm, half), lambda i: (i, g_col + c))

    def res(arr):
        nd = arr.ndim - 1
        return pl.BlockSpec((None,) + arr.shape[1:], lambda i: (layer,) + (0,) * nd,
                            pipeline_mode=pl.Buffered(1))

    return pl.pallas_call(
        _merge_kernel,
        grid=(n // tm,),
        in_specs=[row(d)] + [gate(c) for c in range(6)]
                 + [row(ATTN_W), row(SG_W), row(SSM_W),
                    res(b_gate), res(wa), res(ws), res(wm), res(wo), res(norm2_g)],
        out_specs=[row(d), row(d)],
        out_shape=[jax.ShapeDtypeStruct((n, d), F32), jax.ShapeDtypeStruct((n, d), BF16)],
        compiler_params=_cparams(("parallel",)),
        name="merge_outproj",
    )(x, proj, proj, proj, proj, proj, proj, y_attn, y_sg, y_ssm, b_gate, wa, ws, wm, wo, norm2_g)


def _ffn_up_kernel(h_ref, wg_ref, wv_ref, cwg_ref, cwv_ref, cbg_ref, cbv_ref, o_ref,
                   wb_ref, e_ref, *, tiles_per_seq, row_chunks):
    i = pl.program_id(1)
    tm = h_ref.shape[0]
    tn = o_ref.shape[1]
    tc = tm // row_chunks
    halo = F32_SUBLANES
    nslab = tn // BLK

    @pl.when(i == 0)
    def _():
        wb_ref[:, :tn] = wg_ref[...].astype(BF16)
        wb_ref[:, tn:] = wv_ref[...].astype(BF16)

    @pl.when(i % tiles_per_seq == 0)
    def _():
        e_ref[:, 0:halo, :] = jnp.zeros((2 * nslab, halo, BLK), F32)

    def conv(slab, base, cw_ref, cb_ref, lanes):
        return (cb_ref[:, lanes] + cw_ref[0:1, lanes] * e_ref[slab, base:base + tc, :]
                + cw_ref[1:2, lanes] * e_ref[slab, base - 1:base - 1 + tc, :]
                + cw_ref[2:3, lanes] * e_ref[slab, base - 2:base - 2 + tc, :])

    for r in range(row_chunks):
        rows = slice(r * tc, (r + 1) * tc)
        base = halo + r * tc
        e = _dot(h_ref[rows, :], wb_ref[...])
        for s in range(2 * nslab):
            e_ref[s, base:base + tc, :] = e[:, s * BLK:(s + 1) * BLK]
        for s in range(nslab):
            lanes = slice(s * BLK, (s + 1) * BLK)
            gate = conv(s, base, cwg_ref, cbg_ref, lanes)
            val = conv(nslab + s, base, cwv_ref, cbv_ref, lanes)
            o_ref[rows, lanes] = (_gelu(gate) * val).astype(o_ref.dtype)
    e_ref[:, 0:halo, :] = e_ref[:, tm:tm + halo, :]


def _ffn_up(h, w_up, conv_w, conv_b, layer, seq_len, tm=2048, tn=512, row_chunks=4):
    n, d = h.shape
    nj = D_FF // tn
    wspec = lambda off: pl.BlockSpec((None, d, tn), lambda j, i: (layer, 0, j + off))
    cspec = lambda rows, off: pl.BlockSpec((None, rows, tn), lambda j, i: (layer, 0, j + off))
    return pl.pallas_call(
        functools.partial(_ffn_up_kernel, tiles_per_seq=seq_len // tm, row_chunks=row_chunks),
        grid=(nj, n // tm),
        in_specs=[pl.BlockSpec((tm, d), lambda j, i: (i, 0)),
                  wspec(0), wspec(nj), cspec(3, 0), cspec(3, nj), cspec(1, 0), cspec(1, nj)],
        out_specs=pl.BlockSpec((tm, tn), lambda j, i: (i, j)),
        out_shape=jax.ShapeDtypeStruct((n, D_FF), BF16),
        scratch_shapes=[pltpu.VMEM((d, 2 * tn), BF16),
                        pltpu.VMEM((2 * tn // BLK, F32_SUBLANES + tm, BLK), F32)],
        compiler_params=_cparams(("parallel", "arbitrary")),
        name="ffn_up_conv",
    )(h, w_up, w_up, conv_w, conv_w, conv_b, conv_b)


def _ffn_down_kernel(a_ref, w_ref, x_ref, xo_ref):
    xo_ref[...] = x_ref[...] + _dot(a_ref[...], w_ref[...])


def _ffn_down_norm_kernel(a_ref, w_ref, x_ref, g_ref, xo_ref, h_ref, inv_ref, ssq_ref):
    j = pl.program_id(1)
    xn = x_ref[...] + _dot(a_ref[...], w_ref[...])
    xo_ref[...] = xn
    h_ref[...] = (xn * g_ref[...]).astype(h_ref.dtype)

    @pl.when(j == 0)
    def _():
        ssq_ref[...] = jnp.zeros_like(ssq_ref)

    sq = xn * xn
    part = sq[:, 0:BLK]
    for c in range(1, sq.shape[1] // BLK):
        part = part + sq[:, c * BLK:(c + 1) * BLK]
    ssq_ref[...] += part

    @pl.when(j == pl.num_programs(1) - 1)
    def _():
        ssq = jnp.sum(ssq_ref[...], axis=-1, keepdims=True)
        inv_ref[...] = lax.rsqrt(ssq * (1.0 / D_MODEL) + EPS)


def _ffn_down(act, w_down, x, norm_g, layer, next_layer, tm=1024, tn=256):
    n, d = x.shape
    k = act.shape[1]
    tile = pl.BlockSpec((tm, tn), lambda i, j: (i, j))
    in_specs = [pl.BlockSpec((tm, k), lambda i, j: (i, 0)),
                pl.BlockSpec((None, k, tn), lambda i, j: (layer, 0, j)),
                tile]
    common = dict(grid=(n // tm, d // tn), compiler_params=_cparams(("parallel", "arbitrary")))
    x_shape = jax.ShapeDtypeStruct((n, d), F32)
    if next_layer is None:
        xo = pl.pallas_call(_ffn_down_kernel, in_specs=in_specs, out_specs=tile, out_shape=x_shape,
                            name="ffn_down_last", **common)(act, w_down, x)
        return xo, None, None
    return pl.pallas_call(
        _ffn_down_norm_kernel,
        in_specs=in_specs + [pl.BlockSpec((None, 1, tn), lambda i, j: (next_layer, 0, j))],
        out_specs=[tile, tile, pl.BlockSpec((tm, 1), lambda i, j: (i, 0))],
        out_shape=[x_shape, jax.ShapeDtypeStruct((n, d), BF16), jax.ShapeDtypeStruct((n, 1), F32)],
        scratch_shapes=[pltpu.VMEM((tm, BLK), F32)],
        name="ffn_down",
        **common,
    )(act, w_down, x, norm_g)


def kernel(x, norm1_g, w_in, b_gate, q_norm_g, k_norm_g, attn_sinks, sg_ln_g, sg_ln_b, sg_w, sg_b, ssm_a_re, ssm_a_im, ssm_log_dt, ssm_b_re, ssm_b_im, ssm_c_re, ssm_c_im, ssm_d, ssm_w_glu, ssm_b_glu, w_proj_attn, w_proj_sg, w_proj_ssm, w_out, norm2_g, ffn_w_up, ffn_conv_w, ffn_conv_b, ffn_w_down):
    b, seq_len, d = x.shape
    depth = w_in.shape[0]
    xr = x.reshape(b * seq_len, d)
    ssm_ops = _ssm_operators(ssm_a_re, ssm_a_im, ssm_log_dt, ssm_b_re, ssm_b_im, ssm_c_re, ssm_c_im)
    w_glu, wa, ws, wm, wo, w_down = (w.astype(BF16) for w in (
        ssm_w_glu, w_proj_attn, w_proj_sg, w_proj_ssm, w_out, ffn_w_down))
    rowvec = lambda v: v.astype(F32).reshape(depth, 1, -1)
    ssm_d_r, b_glu_r, b_gate_r, norm1_r, norm2_r, conv_b_r = (rowvec(v) for v in (
        ssm_d, ssm_b_glu, b_gate, norm1_g, norm2_g, ffn_conv_b))

    h, inv = _rmsnorm(xr, norm1_r, 0)
    for l in range(depth):
        proj = _in_proj(h, inv, w_in, l)
        y_attn = _attention(proj, q_norm_g[l], k_norm_g[l], attn_sinks[l], seq_len)
        y_sg = _spatial_gating(proj, sg_ln_g[l], sg_ln_b[l], sg_w[l], sg_b[l])
        y_ssm = _ssm(proj, ssm_ops, ssm_d_r, w_glu, b_glu_r, l, seq_len)
        xr, h2 = _merge(xr, proj, y_attn, y_sg, y_ssm, b_gate_r, wa, ws, wm, wo, norm2_r, l)
        act = _ffn_up(h2, ffn_w_up, ffn_conv_w, conv_b_r, l, seq_len)
        xr, h, inv = _ffn_down(act, w_down, xr, norm1_r, l, l + 1 if l + 1 < depth else None)
    return xr.reshape(b, seq_len, d)
```

```python
import functools
import math

import jax
import jax.numpy as jnp
from jax import lax
from jax.experimental import pallas as pl
from jax.experimental.pallas import tpu as pltpu

F32 = jnp.float32
BF16 = jnp.bfloat16

D_MODEL = 2048
N_Q_HEADS = 8
N_KV_HEADS = 2
Q_PER_KV = N_Q_HEADS // N_KV_HEADS
HEAD_DIM = 128
BLK = 128
ATTN_W = N_Q_HEADS * HEAD_DIM
KV_W = N_KV_HEADS * HEAD_DIM
SG_GROUPS = 4
SG_W = 512
SSM_GROUPS = 32
SSM_CH = 16
SSM_W = 512
SSM_STATE = 64
SSM_SUB = 8
SSM_KT = 4
SSM_NSTATE = SSM_GROUPS * SSM_STATE
MAIN_W = ATTN_W + 2 * KV_W + 2 * SG_W + SSM_W
GATE_W = 3 * D_MODEL
IN_W = MAIN_W + GATE_W
D_FF = 5632
F32_SUBLANES = 8
EPS = 1e-6

VMEM_LIMIT = 56 * 1024 * 1024


def _cparams(sem):
    return pltpu.CompilerParams(dimension_semantics=sem, vmem_limit_bytes=VMEM_LIMIT)


def _gelu(x):
    c = math.sqrt(2.0 / math.pi)
    return 0.5 * x * (1.0 + jnp.tanh(c * (x + 0.044715 * (x * x * x))))


def _sigmoid(x):
    return 1.0 / (1.0 + jnp.exp(-x))


def _dot(a, b):
    return jnp.dot(a, b, preferred_element_type=F32)


def _row_inv_rms(x):
    return lax.rsqrt(jnp.mean(x * x, axis=-1, keepdims=True) + EPS)


def _rmsnorm_kernel(x_ref, g_ref, o_ref, inv_ref):
    x = x_ref[...]
    o_ref[...] = (x * g_ref[...]).astype(o_ref.dtype)
    inv_ref[...] = _row_inv_rms(x)


def _rmsnorm(x, g, layer, tm=512):
    n, d = x.shape
    return pl.pallas_call(
        _rmsnorm_kernel,
        grid=(n // tm,),
        in_specs=[pl.BlockSpec((tm, d), lambda i: (i, 0)),
                  pl.BlockSpec((None, 1, d), lambda i: (layer, 0, 0))],
        out_specs=[pl.BlockSpec((tm, d), lambda i: (i, 0)), pl.BlockSpec((tm, 1), lambda i: (i, 0))],
        out_shape=[jax.ShapeDtypeStruct((n, d), BF16), jax.ShapeDtypeStruct((n, 1), F32)],
        compiler_params=_cparams(("parallel",)),
        name="rmsnorm",
    )(x, g)


def _in_proj_kernel(h_ref, inv_ref, w_ref, o_ref, wb_ref):
    @pl.when(pl.program_id(1) == 0)
    def _():
        wb_ref[...] = w_ref[...].astype(BF16)

    o_ref[...] = (_dot(h_ref[...], wb_ref[...]) * inv_ref[...]).astype(o_ref.dtype)


def _in_proj(h, inv, w_in, layer, tm=1024, tn=1024):
    n, d = h.shape
    width = w_in.shape[2]
    return pl.pallas_call(
        _in_proj_kernel,
        grid=(width // tn, n // tm),
        in_specs=[pl.BlockSpec((tm, d), lambda j, i: (i, 0)),
                  pl.BlockSpec((tm, 1), lambda j, i: (i, 0)),
                  pl.BlockSpec((None, d, tn), lambda j, i: (layer, 0, j))],
        out_specs=pl.BlockSpec((tm, tn), lambda j, i: (i, j)),
        out_shape=jax.ShapeDtypeStruct((n, width), BF16),
        scratch_shapes=[pltpu.VMEM((d, tn), BF16)],
        compiler_params=_cparams(("parallel", "arbitrary")),
        name="in_proj",
    )(h, inv, w_in)


def _attn_kernel(q_ref, k_ref, v_ref, kp_ref, vp_ref, qg_ref, kg_ref, bias_ref, sink_ref, o_ref,
                 *, blocks_per_step, blocks_per_seq):
    i = pl.program_id(0)
    qg = qg_ref[...] * (HEAD_DIM ** -0.5)
    kg = kg_ref[...]
    row = lax.broadcasted_iota(jnp.int32, (BLK, BLK), 0)
    col = lax.broadcasted_iota(jnp.int32, (BLK, BLK), 1)
    cur4 = jnp.concatenate([col <= row] * Q_PER_KV, axis=0)
    ones = jnp.ones((BLK, BLK), BF16)
    ones_win = jnp.ones((2 * BLK, BLK), BF16)

    def rms(x, g):
        ssq = _dot((x * x).astype(BF16), ones)
        return x * lax.rsqrt(ssq * (1.0 / HEAD_DIM) + EPS) * g

    k_prev = None
    for r in range(blocks_per_step):
        rows = slice(r * BLK, (r + 1) * BLK)
        if r == 0:
            has_prev = (i * blocks_per_step) % blocks_per_seq != 0
            k_prev = [rms(kp_ref[:, h * HEAD_DIM:(h + 1) * HEAD_DIM].astype(F32), kg)
                      for h in range(N_KV_HEADS)]
            v_prev = vp_ref[...]
        else:
            has_prev = None
            v_prev = v_ref[(r - 1) * BLK:r * BLK, :]
        k_cur = []
        for h in range(N_KV_HEADS):
            hc = slice(h * HEAD_DIM, (h + 1) * HEAD_DIM)
            k_cur.append(rms(k_ref[rows, hc].astype(F32), kg))
            kwin = jnp.concatenate([k_prev[h], k_cur[h]], axis=0).astype(BF16)
            vwin = jnp.concatenate([v_prev[:, hc], v_ref[rows, hc]], axis=0)
            vext = jnp.concatenate([vwin, ones_win], axis=1)
            q4 = jnp.concatenate(
                [q_ref[rows, (h * Q_PER_KV + g) * HEAD_DIM:(h * Q_PER_KV + g + 1) * HEAD_DIM].astype(F32)
                 for g in range(Q_PER_KV)], axis=0)
            q4 = rms(q4, qg).astype(BF16)
            s2 = lax.dot_general(q4, kwin, (((1,), (1,)), ((), ())),
                                 preferred_element_type=F32)
            s = jnp.where(cur4, s2[:, BLK:], s2[:, :BLK]) - bias_ref[h]
            if has_prev is not None:
                s = jnp.where(jnp.logical_or(cur4, has_prev), s, -jnp.inf)
            sink = sink_ref[h]
            m = jnp.maximum(jnp.max(s, axis=-1, keepdims=True), sink)
            p = jnp.exp(s - m)
            p2 = jnp.concatenate([jnp.where(cur4, 0.0, p), jnp.where(cur4, p, 0.0)],
                                 axis=1).astype(BF16)
            ov = _dot(p2, vext)
            o = ov[:, :BLK] / (ov[:, BLK:] + jnp.exp(sink - m))
            for g in range(Q_PER_KV):
                c0 = (h * Q_PER_KV + g) * HEAD_DIM
                o_ref[rows, c0:c0 + HEAD_DIM] = o[g * BLK:(g + 1) * BLK].astype(o_ref.dtype)
        k_prev = k_cur


def _attention(proj, q_gain, k_gain, sinks, seq_len, blocks_per_step=4):
    n = proj.shape[0]
    tq = blocks_per_step * BLK
    k_col, v_col = ATTN_W // KV_W, ATTN_W // KV_W + 1
    slopes = 2.0 ** (-8.0 * jnp.arange(1, N_Q_HEADS + 1, dtype=F32) / N_Q_HEADS)
    ri = jnp.arange(BLK)[:, None]
    ci = jnp.arange(BLK)[None, :]
    dist = jnp.where(ci <= ri, ri - ci, ri - ci + BLK).astype(F32)
    bias = (slopes[:, None, None] * dist).reshape(N_KV_HEADS, Q_PER_KV * BLK, BLK)
    sink_tile = jnp.broadcast_to(sinks.astype(F32)[:, None, None], (N_Q_HEADS, BLK, BLK))
    sink_tile = sink_tile.reshape(N_KV_HEADS, Q_PER_KV * BLK, BLK)

    def prev_map(col):
        return lambda i: (jnp.maximum(i * blocks_per_step - 1, 0), col)

    head_tile = pl.BlockSpec((N_KV_HEADS, Q_PER_KV * BLK, BLK), lambda i: (0, 0, 0))
    kern = functools.partial(_attn_kernel, blocks_per_step=blocks_per_step,
                             blocks_per_seq=seq_len // BLK)
    return pl.pallas_call(
        kern,
        grid=(n // tq,),
        in_specs=[pl.BlockSpec((tq, ATTN_W), lambda i: (i, 0)),
                  pl.BlockSpec((tq, KV_W), lambda i: (i, k_col)),
                  pl.BlockSpec((tq, KV_W), lambda i: (i, v_col)),
                  pl.BlockSpec((BLK, KV_W), prev_map(k_col)),
                  pl.BlockSpec((BLK, KV_W), prev_map(v_col)),
                  pl.BlockSpec((1, HEAD_DIM), lambda i: (0, 0)),
                  pl.BlockSpec((1, HEAD_DIM), lambda i: (0, 0)),
                  head_tile, head_tile],
        out_specs=pl.BlockSpec((tq, ATTN_W), lambda i: (i, 0)),
        out_shape=jax.ShapeDtypeStruct((n, ATTN_W), BF16),
        compiler_params=_cparams(("parallel",)),
        name="swa_attention",
    )(proj, proj, proj, proj, proj, q_gain.reshape(1, HEAD_DIM).astype(F32),
      k_gain.reshape(1, HEAD_DIM).astype(F32), bias, sink_tile)


def _sg_kernel(u_ref, v_ref, lng_ref, lnb_ref, w_ref, b_ref, o_ref, *, chunks_per_step):
    row = lax.broadcasted_iota(jnp.int32, (BLK, BLK), 0)
    col = lax.broadcasted_iota(jnp.int32, (BLK, BLK), 1)
    tril = col <= row
    for c in range(chunks_per_step):
        rows = slice(c * BLK, (c + 1) * BLK)
        for g in range(SG_GROUPS):
            gc = slice(g * BLK, (g + 1) * BLK)
            zv = _gelu(v_ref[rows, gc].astype(F32))
            mu = jnp.mean(zv, axis=-1, keepdims=True)
            zc = zv - mu
            var = jnp.mean(zc * zc, axis=-1, keepdims=True)
            zn = zc * lax.rsqrt(var + EPS) * lng_ref[:, gc] + lnb_ref[:, gc]
            w = jnp.where(tril, w_ref[g], 0.0).astype(BF16)
            mixed = _dot(w, zn.astype(BF16)) + b_ref[g]
            zu = _gelu(u_ref[rows, gc].astype(F32))
            o_ref[rows, gc] = (zu * mixed).astype(o_ref.dtype)


def _spatial_gating(proj, ln_g, ln_b, w_s, b_s, chunks_per_step=4):
    n = proj.shape[0]
    tq = chunks_per_step * BLK
    u_col = (ATTN_W + 2 * KV_W) // SG_W
    return pl.pallas_call(
        functools.partial(_sg_kernel, chunks_per_step=chunks_per_step),
        grid=(n // tq,),
        in_specs=[pl.BlockSpec((tq, SG_W), lambda i: (i, u_col)),
                  pl.BlockSpec((tq, SG_W), lambda i: (i, u_col + 1)),
                  pl.BlockSpec((1, SG_W), lambda i: (0, 0)),
                  pl.BlockSpec((1, SG_W), lambda i: (0, 0)),
                  pl.BlockSpec((SG_GROUPS, BLK, BLK), lambda i: (0, 0, 0)),
                  pl.BlockSpec((SG_GROUPS, BLK, 1), lambda i: (0, 0, 0))],
        out_specs=pl.BlockSpec((tq, SG_W), lambda i: (i, 0)),
        out_shape=jax.ShapeDtypeStruct((n, SG_W), BF16),
        compiler_params=_cparams(("parallel",)),
        name="spatial_gating",
    )(proj, proj, ln_g.reshape(1, SG_W).astype(F32), ln_b.reshape(1, SG_W).astype(F32),
      w_s.astype(F32), b_s.astype(F32)[..., None])


def _ssm_operators(a_re, a_im, log_dt, b_re, b_im, c_re, c_im):
    hp = lax.Precision.HIGHEST
    p, c, kt, sub = SSM_STATE, SSM_CH, SSM_KT, SSM_SUB
    gl = SSM_GROUPS // kt
    nl = a_re.shape[0]
    a_re, a_im = a_re.astype(F32)[:, None], a_im.astype(F32)[:, None]
    dt = jnp.exp(log_dt.astype(F32))[:, None, :, None]
    k = jnp.arange(sub + 1, dtype=F32)[None, :, None, None]
    mag = jnp.exp(a_re * dt * k)
    pr, pi = mag * jnp.cos(a_im * dt * k), mag * jnp.sin(a_im * dt * k)
    ar, ai, lr, li = pr[:, 1], pi[:, 1], a_re[:, 0], a_im[:, 0]
    den = lr * lr + li * li
    cr = ((ar - 1.0) * lr + ai * li) / den
    ci = (ai * lr - (ar - 1.0) * li) / den
    bbr = cr[..., None] * b_re - ci[..., None] * b_im
    bbi = cr[..., None] * b_im + ci[..., None] * b_re
    bbr_t, bbi_t = jnp.swapaxes(bbr, -1, -2), jnp.swapaxes(bbi, -1, -2)
    car = c_re[:, None] * pr[:, :, :, None, :] - c_im[:, None] * pi[:, :, :, None, :]
    cai = c_re[:, None] * pi[:, :, :, None, :] + c_im[:, None] * pr[:, :, :, None, :]

    same_group = jnp.eye(gl, dtype=F32)

    def expand_cols(x, blocks):
        r, w = x.shape[3], x.shape[5]
        y = x[:, :, :, :, :, None, :] * same_group[None, None, :, None, None, :, None]
        return y.reshape(nl, kt, gl * r, blocks * gl * w)

    kk = (jnp.einsum("xgdp,xlgcp->xlgdc", bbr_t, car[:, :sub], precision=hp)
          - jnp.einsum("xgdp,xlgcp->xlgdc", bbi_t, cai[:, :sub], precision=hp))
    s_idx = jnp.arange(sub)[:, None]
    t_idx = jnp.arange(sub)[None, :]
    lag = jnp.clip(t_idx - s_idx, 0, sub - 1)
    causal = (t_idx >= s_idx).astype(F32)
    kst = kk[:, lag] * causal[None, :, :, None, None, None]
    kst = kst.reshape(nl, sub, sub, kt, gl, c, c).transpose(0, 3, 1, 4, 5, 2, 6)
    toep = jnp.concatenate([expand_cols(kst[:, :, s], sub) for s in range(sub)], axis=2)

    rr, ri = pr[:, sub - 1::-1][:, :, :, None, :], pi[:, sub - 1::-1][:, :, :, None, :]
    wr = rr * bbr_t[:, None] - ri * bbi_t[:, None]
    wi = rr * bbi_t[:, None] + ri * bbr_t[:, None]

    def inject(x):
        x = x.reshape(nl, sub, kt, gl, c, 1, p).transpose(0, 2, 1, 3, 4, 5, 6)
        return jnp.concatenate([expand_cols(x[:, :, s], 1) for s in range(sub)], axis=2)

    w2 = jnp.concatenate([inject(wr), inject(wi)], axis=-1)

    def readout(x):
        x = x.reshape(nl, sub, kt, gl, c, p).transpose(0, 2, 3, 5, 1, 4)
        return expand_cols(x, sub)

    cre = readout(car[:, 1:])
    cim = readout(-cai[:, 1:])
    a8 = jnp.stack([pr[:, sub].reshape(nl, -1), pi[:, sub].reshape(nl, -1)], axis=1)
    return toep.astype(BF16), w2.astype(BF16), cre.astype(BF16), cim.astype(BF16), a8


def _ssm_kernel(x_ref, toep_ref, w2_ref, cre_ref, cim_ref, a8_ref, d_ref, wg_ref, bg_ref, o_ref,
                xf_ref, sloc_ref, st_ref, of_ref, carry_ref, *, tiles_per_seq):
    i = pl.program_id(0)
    tm = x_ref.shape[0]
    nj = tm // SSM_SUB
    ns = SSM_NSTATE
    gw = ns // SSM_KT

    @pl.when(i % tiles_per_seq == 0)
    def _():
        carry_ref[...] = jnp.zeros_like(carry_ref)

    xall = []
    for kt in range(SSM_KT):
        xf_ref[kt] = x_ref[:, kt * BLK:(kt + 1) * BLK].astype(F32)
        xall.append(jnp.concatenate(
            [xf_ref[kt, pl.ds(s, nj, stride=SSM_SUB), :].astype(BF16) for s in range(SSM_SUB)], axis=1))

    for kt in range(SSM_KT):
        acc = _dot(xall[kt], w2_ref[kt])
        sloc_ref[:, kt * gw:(kt + 1) * gw] = acc[:, :gw]
        sloc_ref[:, ns + kt * gw:ns + (kt + 1) * gw] = acc[:, gw:]

    a8r = a8_ref[0:1, :]
    a8i = a8_ref[1:2, :]

    def step(j, st):
        sr, si = st
        st_ref[pl.ds(j, 1), :ns] = sr
        st_ref[pl.ds(j, 1), ns:] = si
        lr = sloc_ref[pl.ds(j, 1), :ns]
        li = sloc_ref[pl.ds(j, 1), ns:]
        return (a8r * sr - a8i * si + lr, a8r * si + a8i * sr + li)

    sr, si = lax.fori_loop(0, nj, step, (carry_ref[0:1, :], carry_ref[1:2, :]), unroll=8)
    carry_ref[0:1, :] = sr
    carry_ref[1:2, :] = si

    stb = st_ref[...].astype(BF16)
    for kt in range(SSM_KT):
        y = (_dot(xall[kt], toep_ref[kt]) + _dot(stb[:, kt * gw:(kt + 1) * gw], cre_ref[kt])
             + _dot(stb[:, ns + kt * gw:ns + (kt + 1) * gw], cim_ref[kt]))
        for t in range(SSM_SUB):
            of_ref[kt, pl.ds(t, nj, stride=SSM_SUB), :] = y[:, t * BLK:(t + 1) * BLK]
    y = jnp.concatenate([of_ref[kt] for kt in range(SSM_KT)], axis=1)
    y = _gelu(y + d_ref[...] * x_ref[...].astype(F32))
    o_ref[...] = (y * _sigmoid(_dot(y.astype(BF16), wg_ref[...]) + bg_ref[...])).astype(o_ref.dtype)


def _ssm(proj, ops, d_skip, w_glu, b_glu, layer, seq_len, tm=1024):
    n = proj.shape[0]
    toep, w2, cre, cim, a8 = ops
    s_col = (MAIN_W - SSM_W) // SSM_W
    nj = tm // SSM_SUB

    def layer_spec(arr):
        nd = arr.ndim - 1
        return pl.BlockSpec((None,) + arr.shape[1:], lambda i: (layer,) + (0,) * nd,
                            pipeline_mode=pl.Buffered(1))

    return pl.pallas_call(
        functools.partial(_ssm_kernel, tiles_per_seq=seq_len // tm),
        grid=(n // tm,),
        in_specs=[pl.BlockSpec((tm, SSM_W), lambda i: (i, s_col)),
                  layer_spec(toep), layer_spec(w2), layer_spec(cre), layer_spec(cim), layer_spec(a8),
                  layer_spec(d_skip), layer_spec(w_glu), layer_spec(b_glu)],
        out_specs=pl.BlockSpec((tm, SSM_W), lambda i: (i, 0)),
        out_shape=jax.ShapeDtypeStruct((n, SSM_W), BF16),
        scratch_shapes=[pltpu.VMEM((SSM_KT, tm, BLK), F32),
                        pltpu.VMEM((nj, 2 * SSM_NSTATE), F32),
                        pltpu.VMEM((nj, 2 * SSM_NSTATE), F32),
                        pltpu.VMEM((SSM_KT, tm, BLK), F32),
                        pltpu.VMEM((2, SSM_NSTATE), F32)],
        compiler_params=_cparams(("arbitrary",)),
        name="s5_ssm_glu",
    )(proj, toep, w2, cre, cim, a8, d_skip, w_glu, b_glu)


def _merge_kernel(x_ref, g0a_ref, g0b_ref, g1a_ref, g1b_ref, g2a_ref, g2b_ref, ya_ref, ys_ref, ym_ref,
                  bg_ref, wa_ref, ws_ref, wm_ref, wo_ref, n2_ref, xo_ref, h_ref):
    d = D_MODEL

    def gate(lo_ref, hi_ref, b):
        logits = jnp.concatenate([lo_ref[...], hi_ref[...]], axis=1).astype(F32)
        return _sigmoid(logits + bg_ref[:, b * d:(b + 1) * d])

    merged = gate(g0a_ref, g0b_ref, 0) * _dot(ya_ref[...], wa_ref[...])
    merged = merged + gate(g1a_ref, g1b_ref, 1) * _dot(ys_ref[...], ws_ref[...])
    merged = merged + gate(g2a_ref, g2b_ref, 2) * _dot(ym_ref[...], wm_ref[...])
    xn = x_ref[...] + _dot(merged.astype(BF16), wo_ref[...])
    xo_ref[...] = xn
    h_ref[...] = (xn * _row_inv_rms(xn) * n2_ref[...]).astype(h_ref.dtype)


def _merge(x, proj, y_attn, y_sg, y_ssm, b_gate, wa, ws, wm, wo, norm2_g, layer, tm=256):
    n, d = x.shape
    half = d // 2
    g_col = MAIN_W // half
    row = lambda w: pl.BlockSpec((tm, w), lambda i: (i, 0))
    gate = lambda c: pl.BlockSpec((tm, half), lambda i: (i, g_col + c))

    def res(arr):
        nd = arr.ndim - 1
        return pl.BlockSpec((None,) + arr.shape[1:], lambda i: (layer,) + (0,) * nd,
                            pipeline_mode=pl.Buffered(1))

    return pl.pallas_call(
        _merge_kernel,
        grid=(n // tm,),
        in_specs=[row(d)] + [gate(c) for c in range(6)]
                 + [row(ATTN_W), row(SG_W), row(SSM_W),
                    res(b_gate), res(wa), res(ws), res(wm), res(wo), res(norm2_g)],
        out_specs=[row(d), row(d)],
        out_shape=[jax.ShapeDtypeStruct((n, d), F32), jax.ShapeDtypeStruct((n, d), BF16)],
        compiler_params=_cparams(("parallel",)),
        name="merge_outproj",
    )(x, proj, proj, proj, proj, proj, proj, y_attn, y_sg, y_ssm, b_gate, wa, ws, wm, wo, norm2_g)


def _ffn_up_kernel(h_ref, wg_ref, wv_ref, cwg_ref, cwv_ref, cbg_ref, cbv_ref, o_ref,
                   wb_ref, e_ref, *, tiles_per_seq, row_chunks):
    i = pl.program_id(1)
    tm = h_ref.shape[0]
    tn = o_ref.shape[1]
    tc = tm // row_chunks
    halo = F32_SUBLANES
    nslab = tn // BLK

    @pl.when(i == 0)
    def _():
        wb_ref[:, :tn] = wg_ref[...].astype(BF16)
        wb_ref[:, tn:] = wv_ref[...].astype(BF16)

    @pl.when(i % tiles_per_seq == 0)
    def _():
        e_ref[:, 0:halo, :] = jnp.zeros((2 * nslab, halo, BLK), F32)

    def conv(slab, base, cw_ref, cb_ref, lanes):
        return (cb_ref[:, lanes] + cw_ref[0:1, lanes] * e_ref[slab, base:base + tc, :]
                + cw_ref[1:2, lanes] * e_ref[slab, base - 1:base - 1 + tc, :]
                + cw_ref[2:3, lanes] * e_ref[slab, base - 2:base - 2 + tc, :])

    for r in range(row_chunks):
        rows = slice(r * tc, (r + 1) * tc)
        base = halo + r * tc
        e = _dot(h_ref[rows, :], wb_ref[...])
        for s in range(2 * nslab):
            e_ref[s, base:base + tc, :] = e[:, s * BLK:(s + 1) * BLK]
        for s in range(nslab):
            lanes = slice(s * BLK, (s + 1) * BLK)
            gate = conv(s, base, cwg_ref, cbg_ref, lanes)
            val = conv(nslab + s, base, cwv_ref, cbv_ref, lanes)
            o_ref[rows, lanes] = (_gelu(gate) * val).astype(o_ref.dtype)
    e_ref[:, 0:halo, :] = e_ref[:, tm:tm + halo, :]


def _ffn_up(h, w_up, conv_w, conv_b, layer, seq_len, tm=2048, tn=512, row_chunks=4):
    n, d = h.shape
    nj = D_FF // tn
    wspec = lambda off: pl.BlockSpec((None, d, tn), lambda j, i: (layer, 0, j + off))
    cspec = lambda rows, off: pl.BlockSpec((None, rows, tn), lambda j, i: (layer, 0, j + off))
    return pl.pallas_call(
        functools.partial(_ffn_up_kernel, tiles_per_seq=seq_len // tm, row_chunks=row_chunks),
        grid=(nj, n // tm),
        in_specs=[pl.BlockSpec((tm, d), lambda j, i: (i, 0)),
                  wspec(0), wspec(nj), cspec(3, 0), cspec(3, nj), cspec(1, 0), cspec(1, nj)],
        out_specs=pl.BlockSpec((tm, tn), lambda j, i: (i, j)),
        out_shape=jax.ShapeDtypeStruct((n, D_FF), BF16),
        scratch_shapes=[pltpu.VMEM((d, 2 * tn), BF16),
                        pltpu.VMEM((2 * tn // BLK, F32_SUBLANES + tm, BLK), F32)],
        compiler_params=_cparams(("parallel", "arbitrary")),
        name="ffn_up_conv",
    )(h, w_up, w_up, conv_w, conv_w, conv_b, conv_b)


def _ffn_down_kernel(a_ref, w_ref, x_ref, xo_ref):
    xo_ref[...] = x_ref[...] + _dot(a_ref[...], w_ref[...])


def _ffn_down_norm_kernel(a_ref, w_ref, x_ref, g_ref, xo_ref, h_ref, inv_ref, ssq_ref):
    j = pl.program_id(1)
    xn = x_ref[...] + _dot(a_ref[...], w_ref[...])
    xo_ref[...] = xn
    h_ref[...] = (xn * g_ref[...]).astype(h_ref.dtype)

    @pl.when(j == 0)
    def _():
        ssq_ref[...] = jnp.zeros_like(ssq_ref)

    sq = xn * xn
    part = sq[:, 0:BLK]
    for c in range(1, sq.shape[1] // BLK):
        part = part + sq[:, c * BLK:(c + 1) * BLK]
    ssq_ref[...] += part

    @pl.when(j == pl.num_programs(1) - 1)
    def _():
        ssq = jnp.sum(ssq_ref[...], axis=-1, keepdims=True)
        inv_ref[...] = lax.rsqrt(ssq * (1.0 / D_MODEL) + EPS)


FFN_DOWN_TN = 512


def _ffn_down_weights(w_down):
    nl, k, d = w_down.shape
    w = w_down.astype(BF16).reshape(nl, k, d // FFN_DOWN_TN, FFN_DOWN_TN)
    return w.transpose(0, 2, 1, 3)


def _ffn_down(act, w_down, x, norm_g, layer, next_layer, tm=1024):
    n, d = x.shape
    k = act.shape[1]
    tn = FFN_DOWN_TN
    tile = pl.BlockSpec((tm, tn), lambda i, j: (i, j))
    in_specs = [pl.BlockSpec((tm, k), lambda i, j: (i, 0)),
                pl.BlockSpec((None, None, k, tn), lambda i, j: (layer, j, 0, 0)),
                tile]
    common = dict(grid=(n // tm, d // tn), compiler_params=_cparams(("parallel", "arbitrary")))
    x_shape = jax.ShapeDtypeStruct((n, d), F32)
    if next_layer is None:
        xo = pl.pallas_call(_ffn_down_kernel, in_specs=in_specs, out_specs=tile, out_shape=x_shape,
                            name="ffn_down_last", **common)(act, w_down, x)
        return xo, None, None
    return pl.pallas_call(
        _ffn_down_norm_kernel,
        in_specs=in_specs + [pl.BlockSpec((None, 1, tn), lambda i, j: (next_layer, 0, j))],
        out_specs=[tile, tile, pl.BlockSpec((tm, 1), lambda i, j: (i, 0))],
        out_shape=[x_shape, jax.ShapeDtypeStruct((n, d), BF16), jax.ShapeDtypeStruct((n, 1), F32)],
        scratch_shapes=[pltpu.VMEM((tm, BLK), F32)],
        name="ffn_down",
        **common,
    )(act, w_down, x, norm_g)


def kernel(x, norm1_g, w_in, b_gate, q_norm_g, k_norm_g, attn_sinks, sg_ln_g, sg_ln_b, sg_w, sg_b, ssm_a_re, ssm_a_im, ssm_log_dt, ssm_b_re, ssm_b_im, ssm_c_re, ssm_c_im, ssm_d, ssm_w_glu, ssm_b_glu, w_proj_attn, w_proj_sg, w_proj_ssm, w_out, norm2_g, ffn_w_up, ffn_conv_w, ffn_conv_b, ffn_w_down):
    b, seq_len, d = x.shape
    depth = w_in.shape[0]
    xr = x.reshape(b * seq_len, d)
    ssm_ops = _ssm_operators(ssm_a_re, ssm_a_im, ssm_log_dt, ssm_b_re, ssm_b_im, ssm_c_re, ssm_c_im)
    w_glu, wa, ws, wm, wo = (w.astype(BF16) for w in (
        ssm_w_glu, w_proj_attn, w_proj_sg, w_proj_ssm, w_out))
    w_down = _ffn_down_weights(ffn_w_down)
    rowvec = lambda v: v.astype(F32).reshape(depth, 1, -1)
    ssm_d_r, b_glu_r, b_gate_r, norm1_r, norm2_r, conv_b_r = (rowvec(v) for v in (
        ssm_d, ssm_b_glu, b_gate, norm1_g, norm2_g, ffn_conv_b))

    h, inv = _rmsnorm(xr, norm1_r, 0)
    for l in range(depth):
        proj = _in_proj(h, inv, w_in, l)
        y_attn = _attention(proj, q_norm_g[l], k_norm_g[l], attn_sinks[l], seq_len)
        y_sg = _spatial_gating(proj, sg_ln_g[l], sg_ln_b[l], sg_w[l], sg_b[l])
        y_ssm = _ssm(proj, ssm_ops, ssm_d_r, w_glu, b_glu_r, l, seq_len)
        xr, h2 = _merge(xr, proj, y_attn, y_sg, y_ssm, b_gate_r, wa, ws, wm, wo, norm2_r, l)
        act = _ffn_up(h2, ffn_w_up, ffn_conv_w, conv_b_r, l, seq_len)
        xr, h, inv = _ffn_down(act, w_down, xr, norm1_r, l, l + 1 if l + 1 < depth else None)
    return xr.reshape(b, seq_len, d)
```

```python
import functools
import math

import jax
import jax.numpy as jnp
from jax import lax
from jax.experimental import pallas as pl
from jax.experimental.pallas import tpu as pltpu

F32 = jnp.float32
BF16 = jnp.bfloat16

D_MODEL = 2048
N_Q_HEADS = 8
N_KV_HEADS = 2
Q_PER_KV = N_Q_HEADS // N_KV_HEADS
HEAD_DIM = 128
BLK = 128
ATTN_W = N_Q_HEADS * HEAD_DIM
KV_W = N_KV_HEADS * HEAD_DIM
SG_GROUPS = 4
SG_W = 512
SSM_GROUPS = 32
SSM_CH = 16
SSM_W = 512
SSM_STATE = 64
SSM_SUB = 8
SSM_KT = 4
SSM_NSTATE = SSM_GROUPS * SSM_STATE
MAIN_W = ATTN_W + 2 * KV_W + 2 * SG_W + SSM_W
GATE_W = 3 * D_MODEL
IN_W = MAIN_W + GATE_W
D_FF = 5632
F32_SUBLANES = 8
EPS = 1e-6

VMEM_LIMIT = 56 * 1024 * 1024


def _cparams(sem):
    return pltpu.CompilerParams(dimension_semantics=sem, vmem_limit_bytes=VMEM_LIMIT)


def _gelu(x):
    c = math.sqrt(2.0 / math.pi)
    return 0.5 * x * (1.0 + jnp.tanh(c * (x + 0.044715 * (x * x * x))))


def _sigmoid(x):
    return 1.0 / (1.0 + jnp.exp(-x))


def _dot(a, b):
    return jnp.dot(a, b, preferred_element_type=F32)


def _row_inv_rms(x):
    return lax.rsqrt(jnp.mean(x * x, axis=-1, keepdims=True) + EPS)


def _rmsnorm_kernel(x_ref, g_ref, o_ref, inv_ref):
    x = x_ref[...]
    o_ref[...] = (x * g_ref[...]).astype(o_ref.dtype)
    inv_ref[...] = _row_inv_rms(x)


def _rmsnorm(x, g, layer, tm=512):
    n, d = x.shape
    return pl.pallas_call(
        _rmsnorm_kernel,
        grid=(n // tm,),
        in_specs=[pl.BlockSpec((tm, d), lambda i: (i, 0)),
                  pl.BlockSpec((None, 1, d), lambda i: (layer, 0, 0))],
        out_specs=[pl.BlockSpec((tm, d), lambda i: (i, 0)), pl.BlockSpec((tm, 1), lambda i: (i, 0))],
        out_shape=[jax.ShapeDtypeStruct((n, d), BF16), jax.ShapeDtypeStruct((n, 1), F32)],
        compiler_params=_cparams(("parallel",)),
        name="rmsnorm",
    )(x, g)


def _in_proj_kernel(h_ref, inv_ref, w_ref, o_ref, wb_ref):
    @pl.when(pl.program_id(1) == 0)
    def _():
        wb_ref[...] = w_ref[...].astype(BF16)

    o_ref[...] = (_dot(h_ref[...], wb_ref[...]) * inv_ref[...]).astype(o_ref.dtype)


def _in_proj(h, inv, w_in, layer, tm=1024, tn=1024):
    n, d = h.shape
    width = w_in.shape[2]
    return pl.pallas_call(
        _in_proj_kernel,
        grid=(width // tn, n // tm),
        in_specs=[pl.BlockSpec((tm, d), lambda j, i: (i, 0)),
                  pl.BlockSpec((tm, 1), lambda j, i: (i, 0)),
                  pl.BlockSpec((None, d, tn), lambda j, i: (layer, 0, j))],
        out_specs=pl.BlockSpec((tm, tn), lambda j, i: (i, j)),
        out_shape=jax.ShapeDtypeStruct((n, width), BF16),
        scratch_shapes=[pltpu.VMEM((d, tn), BF16)],
        compiler_params=_cparams(("parallel", "arbitrary")),
        name="in_proj",
    )(h, inv, w_in)


def _attn_kernel(q_ref, k_ref, v_ref, kp_ref, vp_ref, qg_ref, kg_ref, bias_ref, sink_ref, o_ref,
                 *, blocks_per_step, blocks_per_seq):
    i = pl.program_id(0)
    qg = qg_ref[...] * (HEAD_DIM ** -0.5)
    kg = kg_ref[...]
    row = lax.broadcasted_iota(jnp.int32, (BLK, BLK), 0)
    col = lax.broadcasted_iota(jnp.int32, (BLK, BLK), 1)
    cur4 = jnp.concatenate([col <= row] * Q_PER_KV, axis=0)
    ones = jnp.ones((BLK, BLK), BF16)
    ones_win = jnp.ones((2 * BLK, BLK), BF16)

    def rms(x, g):
        ssq = _dot((x * x).astype(BF16), ones)
        return x * lax.rsqrt(ssq * (1.0 / HEAD_DIM) + EPS) * g

    k_prev = None
    for r in range(blocks_per_step):
        rows = slice(r * BLK, (r + 1) * BLK)
        if r == 0:
            has_prev = (i * blocks_per_step) % blocks_per_seq != 0
            k_prev = [rms(kp_ref[:, h * HEAD_DIM:(h + 1) * HEAD_DIM].astype(F32), kg)
                      for h in range(N_KV_HEADS)]
            v_prev = vp_ref[...]
        else:
            has_prev = None
            v_prev = v_ref[(r - 1) * BLK:r * BLK, :]
        k_cur = []
        for h in range(N_KV_HEADS):
            hc = slice(h * HEAD_DIM, (h + 1) * HEAD_DIM)
            k_cur.append(rms(k_ref[rows, hc].astype(F32), kg))
            kwin = jnp.concatenate([k_prev[h], k_cur[h]], axis=0).astype(BF16)
            vwin = jnp.concatenate([v_prev[:, hc], v_ref[rows, hc]], axis=0)
            vext = jnp.concatenate([vwin, ones_win], axis=1)
            q4 = jnp.concatenate(
                [q_ref[rows, (h * Q_PER_KV + g) * HEAD_DIM:(h * Q_PER_KV + g + 1) * HEAD_DIM].astype(F32)
                 for g in range(Q_PER_KV)], axis=0)
            q4 = rms(q4, qg).astype(BF16)
            s2 = lax.dot_general(q4, kwin, (((1,), (1,)), ((), ())),
                                 preferred_element_type=F32)
            s = jnp.where(cur4, s2[:, BLK:], s2[:, :BLK]) - bias_ref[h]
            if has_prev is not None:
                s = jnp.where(jnp.logical_or(cur4, has_prev), s, -jnp.inf)
            sink = sink_ref[h]
            m = jnp.maximum(jnp.max(s, axis=-1, keepdims=True), sink)
            p = jnp.exp(s - m)
            p2 = jnp.concatenate([jnp.where(cur4, 0.0, p), jnp.where(cur4, p, 0.0)],
                                 axis=1).astype(BF16)
            ov = _dot(p2, vext)
            o = ov[:, :BLK] / (ov[:, BLK:] + jnp.exp(sink - m))
            for g in range(Q_PER_KV):
                c0 = (h * Q_PER_KV + g) * HEAD_DIM
                o_ref[rows, c0:c0 + HEAD_DIM] = o[g * BLK:(g + 1) * BLK].astype(o_ref.dtype)
        k_prev = k_cur


def _attention(proj, q_gain, k_gain, sinks, seq_len, blocks_per_step=4):
    n = proj.shape[0]
    tq = blocks_per_step * BLK
    k_col, v_col = ATTN_W // KV_W, ATTN_W // KV_W + 1
    slopes = 2.0 ** (-8.0 * jnp.arange(1, N_Q_HEADS + 1, dtype=F32) / N_Q_HEADS)
    ri = jnp.arange(BLK)[:, None]
    ci = jnp.arange(BLK)[None, :]
    dist = jnp.where(ci <= ri, ri - ci, ri - ci + BLK).astype(F32)
    bias = (slopes[:, None, None] * dist).reshape(N_KV_HEADS, Q_PER_KV * BLK, BLK)
    sink_tile = jnp.broadcast_to(sinks.astype(F32)[:, None, None], (N_Q_HEADS, BLK, BLK))
    sink_tile = sink_tile.reshape(N_KV_HEADS, Q_PER_KV * BLK, BLK)

    def prev_map(col):
        return lambda i: (jnp.maximum(i * blocks_per_step - 1, 0), col)

    head_tile = pl.BlockSpec((N_KV_HEADS, Q_PER_KV * BLK, BLK), lambda i: (0, 0, 0))
    kern = functools.partial(_attn_kernel, blocks_per_step=blocks_per_step,
                             blocks_per_seq=seq_len // BLK)
    return pl.pallas_call(
        kern,
        grid=(n // tq,),
        in_specs=[pl.BlockSpec((tq, ATTN_W), lambda i: (i, 0)),
                  pl.BlockSpec((tq, KV_W), lambda i: (i, k_col)),
                  pl.BlockSpec((tq, KV_W), lambda i: (i, v_col)),
                  pl.BlockSpec((BLK, KV_W), prev_map(k_col)),
                  pl.BlockSpec((BLK, KV_W), prev_map(v_col)),
                  pl.BlockSpec((1, HEAD_DIM), lambda i: (0, 0)),
                  pl.BlockSpec((1, HEAD_DIM), lambda i: (0, 0)),
                  head_tile, head_tile],
        out_specs=pl.BlockSpec((tq, ATTN_W), lambda i: (i, 0)),
        out_shape=jax.ShapeDtypeStruct((n, ATTN_W), BF16),
        compiler_params=_cparams(("parallel",)),
        name="swa_attention",
    )(proj, proj, proj, proj, proj, q_gain.reshape(1, HEAD_DIM).astype(F32),
      k_gain.reshape(1, HEAD_DIM).astype(F32), bias, sink_tile)


def _sg_kernel(u_ref, v_ref, lng_ref, lnb_ref, w_ref, b_ref, o_ref, *, chunks_per_step):
    row = lax.broadcasted_iota(jnp.int32, (BLK, BLK), 0)
    col = lax.broadcasted_iota(jnp.int32, (BLK, BLK), 1)
    tril = col <= row
    for c in range(chunks_per_step):
        rows = slice(c * BLK, (c + 1) * BLK)
        for g in range(SG_GROUPS):
            gc = slice(g * BLK, (g + 1) * BLK)
            zv = _gelu(v_ref[rows, gc].astype(F32))
            mu = jnp.mean(zv, axis=-1, keepdims=True)
            zc = zv - mu
            var = jnp.mean(zc * zc, axis=-1, keepdims=True)
            zn = zc * lax.rsqrt(var + EPS) * lng_ref[:, gc] + lnb_ref[:, gc]
            w = jnp.where(tril, w_ref[g], 0.0).astype(BF16)
            mixed = _dot(w, zn.astype(BF16)) + b_ref[g]
            zu = _gelu(u_ref[rows, gc].astype(F32))
            o_ref[rows, gc] = (zu * mixed).astype(o_ref.dtype)


def _spatial_gating(proj, ln_g, ln_b, w_s, b_s, chunks_per_step=4):
    n = proj.shape[0]
    tq = chunks_per_step * BLK
    u_col = (ATTN_W + 2 * KV_W) // SG_W
    return pl.pallas_call(
        functools.partial(_sg_kernel, chunks_per_step=chunks_per_step),
        grid=(n // tq,),
        in_specs=[pl.BlockSpec((tq, SG_W), lambda i: (i, u_col)),
                  pl.BlockSpec((tq, SG_W), lambda i: (i, u_col + 1)),
                  pl.BlockSpec((1, SG_W), lambda i: (0, 0)),
                  pl.BlockSpec((1, SG_W), lambda i: (0, 0)),
                  pl.BlockSpec((SG_GROUPS, BLK, BLK), lambda i: (0, 0, 0)),
                  pl.BlockSpec((SG_GROUPS, BLK, 1), lambda i: (0, 0, 0))],
        out_specs=pl.BlockSpec((tq, SG_W), lambda i: (i, 0)),
        out_shape=jax.ShapeDtypeStruct((n, SG_W), BF16),
        compiler_params=_cparams(("parallel",)),
        name="spatial_gating",
    )(proj, proj, ln_g.reshape(1, SG_W).astype(F32), ln_b.reshape(1, SG_W).astype(F32),
      w_s.astype(F32), b_s.astype(F32)[..., None])


def _ssm_operators(a_re, a_im, log_dt, b_re, b_im, c_re, c_im):
    hp = lax.Precision.HIGHEST
    p, c, kt, sub = SSM_STATE, SSM_CH, SSM_KT, SSM_SUB
    gl = SSM_GROUPS // kt
    nl = a_re.shape[0]
    a_re, a_im = a_re.astype(F32)[:, None], a_im.astype(F32)[:, None]
    dt = jnp.exp(log_dt.astype(F32))[:, None, :, None]
    k = jnp.arange(sub + 1, dtype=F32)[None, :, None, None]
    mag = jnp.exp(a_re * dt * k)
    pr, pi = mag * jnp.cos(a_im * dt * k), mag * jnp.sin(a_im * dt * k)
    ar, ai, lr, li = pr[:, 1], pi[:, 1], a_re[:, 0], a_im[:, 0]
    den = lr * lr + li * li
    cr = ((ar - 1.0) * lr + ai * li) / den
    ci = (ai * lr - (ar - 1.0) * li) / den
    bbr = cr[..., None] * b_re - ci[..., None] * b_im
    bbi = cr[..., None] * b_im + ci[..., None] * b_re
    bbr_t, bbi_t = jnp.swapaxes(bbr, -1, -2), jnp.swapaxes(bbi, -1, -2)
    car = c_re[:, None] * pr[:, :, :, None, :] - c_im[:, None] * pi[:, :, :, None, :]
    cai = c_re[:, None] * pi[:, :, :, None, :] + c_im[:, None] * pr[:, :, :, None, :]

    def expand(x, rows_per_group, blocks, width):
        col = jnp.arange(blocks * gl * width)
        src = (col // (gl * width)) * width + col % width
        spread = (jnp.arange(blocks * width)[:, None] == src[None, :]).astype(BF16)
        row_group = (jnp.arange(x.shape[2]) // rows_per_group) % gl
        mask = (row_group[:, None] == ((col // width) % gl)[None, :]).astype(BF16)
        return jnp.einsum("xkrm,mn->xkrn", x.astype(BF16), spread, preferred_element_type=BF16) * mask

    kk = (jnp.einsum("xgdp,xlgcp->xlgdc", bbr_t, car[:, :sub], precision=hp)
          - jnp.einsum("xgdp,xlgcp->xlgdc", bbi_t, cai[:, :sub], precision=hp))
    krow = kk.reshape(nl, sub, kt, gl * c, c).transpose(0, 2, 3, 1, 4).reshape(nl, kt, gl * c, sub * c)
    krow = jnp.pad(krow, ((0, 0), (0, 0), (0, 0), (sub * c, 0)))
    kst = jnp.concatenate([krow[..., (sub - s) * c:(2 * sub - s) * c] for s in range(sub)], axis=2)
    toep = expand(kst, c, sub, c)

    rr, ri = pr[:, sub - 1::-1][:, :, :, None, :], pi[:, sub - 1::-1][:, :, :, None, :]
    wr = rr * bbr_t[:, None] - ri * bbi_t[:, None]
    wi = rr * bbi_t[:, None] + ri * bbr_t[:, None]

    def inject(x):
        x = x.reshape(nl, sub, kt, gl * c, p).transpose(0, 2, 1, 3, 4)
        return expand(x.reshape(nl, kt, sub * gl * c, p), c, 1, p)

    w2 = jnp.concatenate([inject(wr), inject(wi)], axis=-1)

    def readout(x):
        x = x.reshape(nl, sub, kt, gl, c, p).transpose(0, 2, 3, 5, 1, 4)
        return expand(x.reshape(nl, kt, gl * p, sub * c), p, sub, c)

    cre = readout(car[:, 1:])
    cim = readout(-cai[:, 1:])
    a8 = jnp.stack([pr[:, sub].reshape(nl, -1), pi[:, sub].reshape(nl, -1)], axis=1)
    return toep, w2, cre, cim, a8


def _ssm_kernel(x_ref, toep_ref, w2_ref, cre_ref, cim_ref, a8_ref, d_ref, wg_ref, bg_ref, o_ref,
                xf_ref, sloc_ref, st_ref, of_ref, carry_ref, *, tiles_per_seq):
    i = pl.program_id(0)
    tm = x_ref.shape[0]
    nj = tm // SSM_SUB
    ns = SSM_NSTATE
    gw = ns // SSM_KT

    @pl.when(i % tiles_per_seq == 0)
    def _():
        carry_ref[...] = jnp.zeros_like(carry_ref)

    xall = []
    for kt in range(SSM_KT):
        xf_ref[kt] = x_ref[:, kt * BLK:(kt + 1) * BLK].astype(F32)
        xall.append(jnp.concatenate(
            [xf_ref[kt, pl.ds(s, nj, stride=SSM_SUB), :].astype(BF16) for s in range(SSM_SUB)], axis=1))

    for kt in range(SSM_KT):
        acc = _dot(xall[kt], w2_ref[kt])
        sloc_ref[:, kt * gw:(kt + 1) * gw] = acc[:, :gw]
        sloc_ref[:, ns + kt * gw:ns + (kt + 1) * gw] = acc[:, gw:]

    a8r = a8_ref[0:1, :]
    a8i = a8_ref[1:2, :]

    def step(j, st):
        sr, si = st
        st_ref[pl.ds(j, 1), :ns] = sr
        st_ref[pl.ds(j, 1), ns:] = si
        lr = sloc_ref[pl.ds(j, 1), :ns]
        li = sloc_ref[pl.ds(j, 1), ns:]
        return (a8r * sr - a8i * si + lr, a8r * si + a8i * sr + li)

    sr, si = lax.fori_loop(0, nj, step, (carry_ref[0:1, :], carry_ref[1:2, :]), unroll=8)
    carry_ref[0:1, :] = sr
    carry_ref[1:2, :] = si

    stb = st_ref[...].astype(BF16)
    for kt in range(SSM_KT):
        y = (_dot(xall[kt], toep_ref[kt]) + _dot(stb[:, kt * gw:(kt + 1) * gw], cre_ref[kt])
             + _dot(stb[:, ns + kt * gw:ns + (kt + 1) * gw], cim_ref[kt]))
        for t in range(SSM_SUB):
            of_ref[kt, pl.ds(t, nj, stride=SSM_SUB), :] = y[:, t * BLK:(t + 1) * BLK]
    y = jnp.concatenate([of_ref[kt] for kt in range(SSM_KT)], axis=1)
    y = _gelu(y + d_ref[...] * x_ref[...].astype(F32))
    o_ref[...] = (y * _sigmoid(_dot(y.astype(BF16), wg_ref[...]) + bg_ref[...])).astype(o_ref.dtype)


def _ssm(proj, ops, d_skip, w_glu, b_glu, layer, seq_len, tm=1024):
    n = proj.shape[0]
    toep, w2, cre, cim, a8 = ops
    s_col = (MAIN_W - SSM_W) // SSM_W
    nj = tm // SSM_SUB

    def layer_spec(arr):
        nd = arr.ndim - 1
        return pl.BlockSpec((None,) + arr.shape[1:], lambda i: (layer,) + (0,) * nd,
                            pipeline_mode=pl.Buffered(1))

    return pl.pallas_call(
        functools.partial(_ssm_kernel, tiles_per_seq=seq_len // tm),
        grid=(n // tm,),
        in_specs=[pl.BlockSpec((tm, SSM_W), lambda i: (i, s_col)),
                  layer_spec(toep), layer_spec(w2), layer_spec(cre), layer_spec(cim), layer_spec(a8),
                  layer_spec(d_skip), layer_spec(w_glu), layer_spec(b_glu)],
        out_specs=pl.BlockSpec((tm, SSM_W), lambda i: (i, 0)),
        out_shape=jax.ShapeDtypeStruct((n, SSM_W), BF16),
        scratch_shapes=[pltpu.VMEM((SSM_KT, tm, BLK), F32),
                        pltpu.VMEM((nj, 2 * SSM_NSTATE), F32),
                        pltpu.VMEM((nj, 2 * SSM_NSTATE), F32),
                        pltpu.VMEM((SSM_KT, tm, BLK), F32),
                        pltpu.VMEM((2, SSM_NSTATE), F32)],
        compiler_params=_cparams(("arbitrary",)),
        name="s5_ssm_glu",
    )(proj, toep, w2, cre, cim, a8, d_skip, w_glu, b_glu)


def _merge_kernel(x_ref, g0a_ref, g0b_ref, g1a_ref, g1b_ref, g2a_ref, g2b_ref, ya_ref, ys_ref, ym_ref,
                  bg_ref, wa_ref, ws_ref, wm_ref, wo_ref, n2_ref, xo_ref, h_ref):
    d = D_MODEL

    def gate(lo_ref, hi_ref, b):
        logits = jnp.concatenate([lo_ref[...], hi_ref[...]], axis=1).astype(F32)
        return _sigmoid(logits + bg_ref[:, b * d:(b + 1) * d])

    merged = gate(g0a_ref, g0b_ref, 0) * _dot(ya_ref[...], wa_ref[...])
    merged = merged + gate(g1a_ref, g1b_ref, 1) * _dot(ys_ref[...], ws_ref[...])
    merged = merged + gate(g2a_ref, g2b_ref, 2) * _dot(ym_ref[...], wm_ref[...])
    xn = x_ref[...] + _dot(merged.astype(BF16), wo_ref[...])
    xo_ref[...] = xn
    h_ref[...] = (xn * _row_inv_rms(xn) * n2_ref[...]).astype(h_ref.dtype)


def _merge(x, proj, y_attn, y_sg, y_ssm, b_gate, wa, ws, wm, wo, norm2_g, layer, tm=256):
    n, d = x.shape
    half = d // 2
    g_col = MAIN_W // half
    row = lambda w: pl.BlockSpec((tm, w), lambda i: (i, 0))
    gate = lambda c: pl.BlockSpec((tm, half), lambda i: (i, g_col + c))

    def res(arr):
        nd = arr.ndim - 1
        return pl.BlockSpec((None,) + arr.shape[1:], lambda i: (layer,) + (0,) * nd,
                            pipeline_mode=pl.Buffered(1))

    return pl.pallas_call(
        _merge_kernel,
        grid=(n // tm,),
        in_specs=[row(d)] + [gate(c) for c in range(6)]
                 + [row(ATTN_W), row(SG_W), row(SSM_W),
                    res(b_gate), res(wa), res(ws), res(wm), res(wo), res(norm2_g)],
        out_specs=[row(d), row(d)],
        out_shape=[jax.ShapeDtypeStruct((n, d), F32), jax.ShapeDtypeStruct((n, d), BF16)],
        compiler_params=_cparams(("parallel",)),
        name="merge_outproj",
    )(x, proj, proj, proj, proj, proj, proj, y_attn, y_sg, y_ssm, b_gate, wa, ws, wm, wo, norm2_g)


def _ffn_up_kernel(h_ref, wg_ref, wv_ref, cwg_ref, cwv_ref, cbg_ref, cbv_ref, o_ref,
                   wb_ref, e_ref, *, tiles_per_seq, row_chunks):
    i = pl.program_id(1)
    tm = h_ref.shape[0]
    tn = o_ref.shape[1]
    tc = tm // row_chunks
    halo = F32_SUBLANES
    nslab = tn // BLK

    @pl.when(i == 0)
    def _():
        wb_ref[:, :tn] = wg_ref[...].astype(BF16)
        wb_ref[:, tn:] = wv_ref[...].astype(BF16)

    @pl.when(i % tiles_per_seq == 0)
    def _():
        e_ref[:, 0:halo, :] = jnp.zeros((2 * nslab, halo, BLK), F32)

    def conv(slab, base, cw_ref, cb_ref, lanes):
        return (cb_ref[:, lanes] + cw_ref[0:1, lanes] * e_ref[slab, base:base + tc, :]
                + cw_ref[1:2, lanes] * e_ref[slab, base - 1:base - 1 + tc, :]
                + cw_ref[2:3, lanes] * e_ref[slab, base - 2:base - 2 + tc, :])

    for r in range(row_chunks):
        rows = slice(r * tc, (r + 1) * tc)
        base = halo + r * tc
        e = _dot(h_ref[rows, :], wb_ref[...])
        for s in range(2 * nslab):
            e_ref[s, base:base + tc, :] = e[:, s * BLK:(s + 1) * BLK]
        for s in range(nslab):
            lanes = slice(s * BLK, (s + 1) * BLK)
            gate = conv(s, base, cwg_ref, cbg_ref, lanes)
            val = conv(nslab + s, base, cwv_ref, cbv_ref, lanes)
            o_ref[rows, lanes] = (_gelu(gate) * val).astype(o_ref.dtype)
    e_ref[:, 0:halo, :] = e_ref[:, tm:tm + halo, :]


def _ffn_up(h, w_up, conv_w, conv_b, layer, seq_len, tm=2048, tn=512, row_chunks=4):
    n, d = h.shape
    nj = D_FF // tn
    wspec = lambda off: pl.BlockSpec((None, d, tn), lambda j, i: (layer, 0, j + off))
    cspec = lambda rows, off: pl.BlockSpec((None, rows, tn), lambda j, i: (layer, 0, j + off))
    return pl.pallas_call(
        functools.partial(_ffn_up_kernel, tiles_per_seq=seq_len // tm, row_chunks=row_chunks),
        grid=(nj, n // tm),
        in_specs=[pl.BlockSpec((tm, d), lambda j, i: (i, 0)),
                  wspec(0), wspec(nj), cspec(3, 0), cspec(3, nj), cspec(1, 0), cspec(1, nj)],
        out_specs=pl.BlockSpec((tm, tn), lambda j, i: (i, j)),
        out_shape=jax.ShapeDtypeStruct((n, D_FF), BF16),
        scratch_shapes=[pltpu.VMEM((d, 2 * tn), BF16),
                        pltpu.VMEM((2 * tn // BLK, F32_SUBLANES + tm, BLK), F32)],
        compiler_params=_cparams(("parallel", "arbitrary")),
        name="ffn_up_conv",
    )(h, w_up, w_up, conv_w, conv_w, conv_b, conv_b)


def _ffn_down_kernel(a_ref, w_ref, x_ref, xo_ref):
    xo_ref[...] = x_ref[...] + _dot(a_ref[...], w_ref[...])


def _ffn_down_norm_kernel(a_ref, w_ref, x_ref, g_ref, xo_ref, h_ref, inv_ref, ssq_ref):
    j = pl.program_id(1)
    xn = x_ref[...] + _dot(a_ref[...], w_ref[...])
    xo_ref[...] = xn
    h_ref[...] = (xn * g_ref[...]).astype(h_ref.dtype)

    @pl.when(j == 0)
    def _():
        ssq_ref[...] = jnp.zeros_like(ssq_ref)

    sq = xn * xn
    part = sq[:, 0:BLK]
    for c in range(1, sq.shape[1] // BLK):
        part = part + sq[:, c * BLK:(c + 1) * BLK]
    ssq_ref[...] += part

    @pl.when(j == pl.num_programs(1) - 1)
    def _():
        ssq = jnp.sum(ssq_ref[...], axis=-1, keepdims=True)
        inv_ref[...] = lax.rsqrt(ssq * (1.0 / D_MODEL) + EPS)


FFN_DOWN_TN = 512


def _cast_kernel(w_ref, o_ref):
    o_ref[...] = w_ref[...].astype(o_ref.dtype)


def _ffn_down_weights(w_down, row_blocks=4):
    nl, k, d = w_down.shape
    tn = FFN_DOWN_TN
    tk = k // row_blocks
    return pl.pallas_call(
        _cast_kernel,
        grid=(nl, d // tn, row_blocks),
        in_specs=[pl.BlockSpec((None, tk, tn), lambda l, j, r: (l, r, j))],
        out_specs=pl.BlockSpec((None, None, tk, tn), lambda l, j, r: (l, j, r, 0)),
        out_shape=jax.ShapeDtypeStruct((nl, d // tn, k, tn), BF16),
        compiler_params=_cparams(("parallel", "parallel", "parallel")),
        name="ffn_down_weight_tiles",
    )(w_down)


def _ffn_down(act, w_down, x, norm_g, layer, next_layer, tm=1024):
    n, d = x.shape
    k = act.shape[1]
    tn = FFN_DOWN_TN
    tile = pl.BlockSpec((tm, tn), lambda i, j: (i, j))
    in_specs = [pl.BlockSpec((tm, k), lambda i, j: (i, 0)),
                pl.BlockSpec((None, None, k, tn), lambda i, j: (layer, j, 0, 0)),
                tile]
    common = dict(grid=(n // tm, d // tn), compiler_params=_cparams(("parallel", "arbitrary")))
    x_shape = jax.ShapeDtypeStruct((n, d), F32)
    if next_layer is None:
        xo = pl.pallas_call(_ffn_down_kernel, in_specs=in_specs, out_specs=tile, out_shape=x_shape,
                            name="ffn_down_last", **common)(act, w_down, x)
        return xo, None, None
    return pl.pallas_call(
        _ffn_down_norm_kernel,
        in_specs=in_specs + [pl.BlockSpec((None, 1, tn), lambda i, j: (next_layer, 0, j))],
        out_specs=[tile, tile, pl.BlockSpec((tm, 1), lambda i, j: (i, 0))],
        out_shape=[x_shape, jax.ShapeDtypeStruct((n, d), BF16), jax.ShapeDtypeStruct((n, 1), F32)],
        scratch_shapes=[pltpu.VMEM((tm, BLK), F32)],
        name="ffn_down",
        **common,
    )(act, w_down, x, norm_g)


def kernel(x, norm1_g, w_in, b_gate, q_norm_g, k_norm_g, attn_sinks, sg_ln_g, sg_ln_b, sg_w, sg_b, ssm_a_re, ssm_a_im, ssm_log_dt, ssm_b_re, ssm_b_im, ssm_c_re, ssm_c_im, ssm_d, ssm_w_glu, ssm_b_glu, w_proj_attn, w_proj_sg, w_proj_ssm, w_out, norm2_g, ffn_w_up, ffn_conv_w, ffn_conv_b, ffn_w_down):
    b, seq_len, d = x.shape
    depth = w_in.shape[0]
    xr = x.reshape(b * seq_len, d)
    ssm_ops = _ssm_operators(ssm_a_re, ssm_a_im, ssm_log_dt, ssm_b_re, ssm_b_im, ssm_c_re, ssm_c_im)
    w_glu, wa, ws, wm, wo = (w.astype(BF16) for w in (
        ssm_w_glu, w_proj_attn, w_proj_sg, w_proj_ssm, w_out))
    w_down = _ffn_down_weights(ffn_w_down)
    rowvec = lambda v: v.astype(F32).reshape(depth, 1, -1)
    ssm_d_r, b_glu_r, b_gate_r, norm1_r, norm2_r, conv_b_r = (rowvec(v) for v in (
        ssm_d, ssm_b_glu, b_gate, norm1_g, norm2_g, ffn_conv_b))

    h, inv = _rmsnorm(xr, norm1_r, 0)
    for l in range(depth):
        proj = _in_proj(h, inv, w_in, l)
        y_attn = _attention(proj, q_norm_g[l], k_norm_g[l], attn_sinks[l], seq_len)
        y_sg = _spatial_gating(proj, sg_ln_g[l], sg_ln_b[l], sg_w[l], sg_b[l])
        y_ssm = _ssm(proj, ssm_ops, ssm_d_r, w_glu, b_glu_r, l, seq_len)
        xr, h2 = _merge(xr, proj, y_attn, y_sg, y_ssm, b_gate_r, wa, ws, wm, wo, norm2_r, l)
        act = _ffn_up(h2, ffn_w_up, ffn_conv_w, conv_b_r, l, seq_len)
        xr, h, inv = _ffn_down(act, w_down, xr, norm1_r, l, l + 1 if l + 1 < depth else None)
    return xr.reshape(b, seq_len, d)
```

```python
import functools
import math

import jax
import jax.numpy as jnp
from jax import lax
from jax.experimental import pallas as pl
from jax.experimental.pallas import tpu as pltpu

F32 = jnp.float32
BF16 = jnp.bfloat16

D_MODEL = 2048
N_Q_HEADS = 8
N_KV_HEADS = 2
Q_PER_KV = N_Q_HEADS // N_KV_HEADS
HEAD_DIM = 128
BLK = 128
ATTN_W = N_Q_HEADS * HEAD_DIM
KV_W = N_KV_HEADS * HEAD_DIM
SG_GROUPS = 4
SG_W = 512
SSM_GROUPS = 32
SSM_CH = 16
SSM_W = 512
SSM_STATE = 64
SSM_SUB = 8
SSM_KT = 4
SSM_NSTATE = SSM_GROUPS * SSM_STATE
MAIN_W = ATTN_W + 2 * KV_W + 2 * SG_W + SSM_W
GATE_W = 3 * D_MODEL
IN_W = MAIN_W + GATE_W
D_FF = 5632
F32_SUBLANES = 8
EPS = 1e-6

VMEM_LIMIT = 56 * 1024 * 1024


def _cparams(sem):
    return pltpu.CompilerParams(dimension_semantics=sem, vmem_limit_bytes=VMEM_LIMIT)


def _gelu(x):
    c = math.sqrt(2.0 / math.pi)
    return 0.5 * x * (1.0 + jnp.tanh(c * (x + 0.044715 * (x * x * x))))


def _sigmoid(x):
    return 1.0 / (1.0 + jnp.exp(-x))


def _dot(a, b):
    return jnp.dot(a, b, preferred_element_type=F32)


def _row_inv_rms(x):
    return lax.rsqrt(jnp.mean(x * x, axis=-1, keepdims=True) + EPS)


def _rmsnorm_kernel(x_ref, g_ref, o_ref, inv_ref):
    x = x_ref[...]
    o_ref[...] = (x * g_ref[...]).astype(o_ref.dtype)
    inv_ref[...] = _row_inv_rms(x)


def _rmsnorm(x, g, layer, tm=1024):
    n, d = x.shape
    return pl.pallas_call(
        _rmsnorm_kernel,
        grid=(n // tm,),
        in_specs=[pl.BlockSpec((tm, d), lambda i: (i, 0)),
                  pl.BlockSpec((None, 1, d), lambda i: (layer, 0, 0))],
        out_specs=[pl.BlockSpec((tm, d), lambda i: (i, 0)), pl.BlockSpec((tm, 1), lambda i: (i, 0))],
        out_shape=[jax.ShapeDtypeStruct((n, d), BF16), jax.ShapeDtypeStruct((n, 1), F32)],
        compiler_params=_cparams(("parallel",)),
        name="rmsnorm",
    )(x, g)


def _in_proj_kernel(h_ref, inv_ref, w_ref, o_ref, wb_ref):
    @pl.when(pl.program_id(1) == 0)
    def _():
        wb_ref[...] = w_ref[...].astype(BF16)

    o_ref[...] = (_dot(h_ref[...], wb_ref[...]) * inv_ref[...]).astype(o_ref.dtype)


def _in_proj(h, inv, w_in, layer, tm=1024, tn=1024):
    n, d = h.shape
    width = w_in.shape[2]
    return pl.pallas_call(
        _in_proj_kernel,
        grid=(width // tn, n // tm),
        in_specs=[pl.BlockSpec((tm, d), lambda j, i: (i, 0)),
                  pl.BlockSpec((tm, 1), lambda j, i: (i, 0)),
                  pl.BlockSpec((None, d, tn), lambda j, i: (layer, 0, j))],
        out_specs=pl.BlockSpec((tm, tn), lambda j, i: (i, j)),
        out_shape=jax.ShapeDtypeStruct((n, width), BF16),
        scratch_shapes=[pltpu.VMEM((d, tn), BF16)],
        compiler_params=_cparams(("parallel", "arbitrary")),
        name="in_proj",
    )(h, inv, w_in)


def _attn_kernel(q_ref, k_ref, v_ref, kp_ref, vp_ref, qg_ref, kg_ref, bias_ref, sink_ref, o_ref,
                 *, blocks_per_step, blocks_per_seq):
    i = pl.program_id(0)
    qg = qg_ref[...] * (HEAD_DIM ** -0.5)
    kg = kg_ref[...]
    row = lax.broadcasted_iota(jnp.int32, (BLK, BLK), 0)
    col = lax.broadcasted_iota(jnp.int32, (BLK, BLK), 1)
    cur4 = jnp.concatenate([col <= row] * Q_PER_KV, axis=0)
    ones = jnp.ones((BLK, BLK), BF16)
    ones_win = jnp.ones((2 * BLK, BLK), BF16)

    def rms(x, g):
        ssq = _dot((x * x).astype(BF16), ones)
        return x * lax.rsqrt(ssq * (1.0 / HEAD_DIM) + EPS) * g

    k_prev = None
    for r in range(blocks_per_step):
        rows = slice(r * BLK, (r + 1) * BLK)
        if r == 0:
            has_prev = (i * blocks_per_step) % blocks_per_seq != 0
            k_prev = [rms(kp_ref[:, h * HEAD_DIM:(h + 1) * HEAD_DIM].astype(F32), kg)
                      for h in range(N_KV_HEADS)]
            v_prev = vp_ref[...]
        else:
            has_prev = None
            v_prev = v_ref[(r - 1) * BLK:r * BLK, :]
        k_cur = []
        for h in range(N_KV_HEADS):
            hc = slice(h * HEAD_DIM, (h + 1) * HEAD_DIM)
            k_cur.append(rms(k_ref[rows, hc].astype(F32), kg))
            kwin = jnp.concatenate([k_prev[h], k_cur[h]], axis=0).astype(BF16)
            vwin = jnp.concatenate([v_prev[:, hc], v_ref[rows, hc]], axis=0)
            vext = jnp.concatenate([vwin, ones_win], axis=1)
            q4 = jnp.concatenate(
                [q_ref[rows, (h * Q_PER_KV + g) * HEAD_DIM:(h * Q_PER_KV + g + 1) * HEAD_DIM].astype(F32)
                 for g in range(Q_PER_KV)], axis=0)
            q4 = rms(q4, qg).astype(BF16)
            s2 = lax.dot_general(q4, kwin, (((1,), (1,)), ((), ())),
                                 preferred_element_type=F32)
            s = jnp.where(cur4, s2[:, BLK:], s2[:, :BLK]) - bias_ref[h]
            if has_prev is not None:
                s = jnp.where(jnp.logical_or(cur4, has_prev), s, -jnp.inf)
            sink = sink_ref[h]
            m = jnp.maximum(jnp.max(s, axis=-1, keepdims=True), sink)
            p = jnp.exp(s - m)
            p2 = jnp.concatenate([jnp.where(cur4, 0.0, p), jnp.where(cur4, p, 0.0)],
                                 axis=1).astype(BF16)
            ov = _dot(p2, vext)
            o = ov[:, :BLK] / (ov[:, BLK:] + jnp.exp(sink - m))
            for g in range(Q_PER_KV):
                c0 = (h * Q_PER_KV + g) * HEAD_DIM
                o_ref[rows, c0:c0 + HEAD_DIM] = o[g * BLK:(g + 1) * BLK].astype(o_ref.dtype)
        k_prev = k_cur


def _sg_kernel(u_ref, v_ref, lng_ref, lnb_ref, w_ref, b_ref, o_ref, *, chunks_per_step):
    row = lax.broadcasted_iota(jnp.int32, (BLK, BLK), 0)
    col = lax.broadcasted_iota(jnp.int32, (BLK, BLK), 1)
    tril = col <= row
    for c in range(chunks_per_step):
        rows = slice(c * BLK, (c + 1) * BLK)
        for g in range(SG_GROUPS):
            gc = slice(g * BLK, (g + 1) * BLK)
            zv = _gelu(v_ref[rows, gc].astype(F32))
            mu = jnp.mean(zv, axis=-1, keepdims=True)
            zc = zv - mu
            var = jnp.mean(zc * zc, axis=-1, keepdims=True)
            zn = zc * lax.rsqrt(var + EPS) * lng_ref[:, gc] + lnb_ref[:, gc]
            w = jnp.where(tril, w_ref[g], 0.0).astype(BF16)
            mixed = _dot(w, zn.astype(BF16)) + b_ref[g]
            zu = _gelu(u_ref[rows, gc].astype(F32))
            o_ref[rows, gc] = (zu * mixed).astype(o_ref.dtype)


def _ssm_operators(a_re, a_im, log_dt, b_re, b_im, c_re, c_im):
    hp = lax.Precision.HIGHEST
    p, c, kt, sub = SSM_STATE, SSM_CH, SSM_KT, SSM_SUB
    gl = SSM_GROUPS // kt
    nl = a_re.shape[0]
    a_re, a_im = a_re.astype(F32)[:, None], a_im.astype(F32)[:, None]
    dt = jnp.exp(log_dt.astype(F32))[:, None, :, None]
    k = jnp.arange(sub + 1, dtype=F32)[None, :, None, None]
    mag = jnp.exp(a_re * dt * k)
    pr, pi = mag * jnp.cos(a_im * dt * k), mag * jnp.sin(a_im * dt * k)
    ar, ai, lr, li = pr[:, 1], pi[:, 1], a_re[:, 0], a_im[:, 0]
    den = lr * lr + li * li
    cr = ((ar - 1.0) * lr + ai * li) / den
    ci = (ai * lr - (ar - 1.0) * li) / den
    bbr = cr[..., None] * b_re - ci[..., None] * b_im
    bbi = cr[..., None] * b_im + ci[..., None] * b_re
    bbr_t, bbi_t = jnp.swapaxes(bbr, -1, -2), jnp.swapaxes(bbi, -1, -2)
    car = c_re[:, None] * pr[:, :, :, None, :] - c_im[:, None] * pi[:, :, :, None, :]
    cai = c_re[:, None] * pi[:, :, :, None, :] + c_im[:, None] * pr[:, :, :, None, :]

    def expand(x, rows_per_group, blocks, width):
        col = jnp.arange(blocks * gl * width)
        src = (col // (gl * width)) * width + col % width
        spread = (jnp.arange(blocks * width)[:, None] == src[None, :]).astype(BF16)
        row_group = (jnp.arange(x.shape[2]) // rows_per_group) % gl
        mask = (row_group[:, None] == ((col // width) % gl)[None, :]).astype(BF16)
        return jnp.einsum("xkrm,mn->xkrn", x.astype(BF16), spread, preferred_element_type=BF16) * mask

    kk = (jnp.einsum("xgdp,xlgcp->xlgdc", bbr_t, car[:, :sub], precision=hp)
          - jnp.einsum("xgdp,xlgcp->xlgdc", bbi_t, cai[:, :sub], precision=hp))
    krow = kk.reshape(nl, sub, kt, gl * c, c).transpose(0, 2, 3, 1, 4).reshape(nl, kt, gl * c, sub * c)
    krow = jnp.pad(krow, ((0, 0), (0, 0), (0, 0), (sub * c, 0)))
    kst = jnp.concatenate([krow[..., (sub - s) * c:(2 * sub - s) * c] for s in range(sub)], axis=2)
    toep = expand(kst, c, sub, c)

    rr, ri = pr[:, sub - 1::-1][:, :, :, None, :], pi[:, sub - 1::-1][:, :, :, None, :]
    wr = rr * bbr_t[:, None] - ri * bbi_t[:, None]
    wi = rr * bbi_t[:, None] + ri * bbr_t[:, None]

    def inject(x):
        x = x.reshape(nl, sub, kt, gl * c, p).transpose(0, 2, 1, 3, 4)
        return expand(x.reshape(nl, kt, sub * gl * c, p), c, 1, p)

    w2 = jnp.concatenate([inject(wr), inject(wi)], axis=-1)

    def readout(x):
        x = x.reshape(nl, sub, kt, gl, c, p).transpose(0, 2, 3, 5, 1, 4)
        return expand(x.reshape(nl, kt, gl * p, sub * c), p, sub, c)

    cre = readout(car[:, 1:])
    cim = readout(-cai[:, 1:])
    a8 = jnp.stack([pr[:, sub].reshape(nl, -1), pi[:, sub].reshape(nl, -1)], axis=1)
    return toep, w2, cre, cim, a8


def _ssm_kernel(x_ref, toep_ref, w2_ref, cre_ref, cim_ref, a8_ref, d_ref, wg_ref, bg_ref, o_ref,
                slab_ref, st_ref, carry_ref, *, tiles_per_seq):
    i = pl.program_id(0)
    tm = x_ref.shape[0]
    nj = tm // SSM_SUB
    ns = SSM_NSTATE
    gw = ns // SSM_KT

    @pl.when(i % tiles_per_seq == 0)
    def _():
        carry_ref[...] = jnp.zeros_like(carry_ref)

    xall = []
    for kt in range(SSM_KT):
        slab_ref[kt] = x_ref[:, kt * BLK:(kt + 1) * BLK].astype(F32)
        xall.append(jnp.concatenate(
            [slab_ref[kt, pl.ds(s, nj, stride=SSM_SUB), :].astype(BF16) for s in range(SSM_SUB)], axis=1))

    for kt in range(SSM_KT):
        acc = _dot(xall[kt], w2_ref[kt])
        st_ref[:, kt * gw:(kt + 1) * gw] = acc[:, :gw]
        st_ref[:, ns + kt * gw:ns + (kt + 1) * gw] = acc[:, gw:]

    a8r = a8_ref[0:1, :]
    a8i = a8_ref[1:2, :]

    def step(j, st):
        sr, si = st
        lr = st_ref[pl.ds(j, 1), :ns]
        li = st_ref[pl.ds(j, 1), ns:]
        st_ref[pl.ds(j, 1), :ns] = sr
        st_ref[pl.ds(j, 1), ns:] = si
        return (a8r * sr - a8i * si + lr, a8r * si + a8i * sr + li)

    sr, si = lax.fori_loop(0, nj, step, (carry_ref[0:1, :], carry_ref[1:2, :]), unroll=8)
    carry_ref[0:1, :] = sr
    carry_ref[1:2, :] = si

    stb = st_ref[...].astype(BF16)
    for kt in range(SSM_KT):
        y = (_dot(xall[kt], toep_ref[kt]) + _dot(stb[:, kt * gw:(kt + 1) * gw], cre_ref[kt])
             + _dot(stb[:, ns + kt * gw:ns + (kt + 1) * gw], cim_ref[kt]))
        for t in range(SSM_SUB):
            slab_ref[kt, pl.ds(t, nj, stride=SSM_SUB), :] = y[:, t * BLK:(t + 1) * BLK]
    y = jnp.concatenate([slab_ref[kt] for kt in range(SSM_KT)], axis=1)
    y = _gelu(y + d_ref[...] * x_ref[...].astype(F32))
    o_ref[...] = (y * _sigmoid(_dot(y.astype(BF16), wg_ref[...]) + bg_ref[...])).astype(o_ref.dtype)


MIX_BLOCKS = 8


def _mixers_kernel(q_ref, k_ref, v_ref, kp_ref, vp_ref, qg_ref, kg_ref, bias_ref, sink_ref,
                   u_ref, sv_ref, lng_ref, lnb_ref, ws_ref, bs_ref,
                   x_ref, toep_ref, w2_ref, cre_ref, cim_ref, a8_ref, d_ref, wg_ref, bg_ref,
                   ya_ref, ysg_ref, yssm_ref, slab_ref, st_ref, carry_ref, *, blocks_per_seq):
    _ssm_kernel(x_ref, toep_ref, w2_ref, cre_ref, cim_ref, a8_ref, d_ref, wg_ref, bg_ref, yssm_ref,
                slab_ref, st_ref, carry_ref, tiles_per_seq=blocks_per_seq // MIX_BLOCKS)
    _attn_kernel(q_ref, k_ref, v_ref, kp_ref, vp_ref, qg_ref, kg_ref, bias_ref, sink_ref, ya_ref,
                 blocks_per_step=MIX_BLOCKS, blocks_per_seq=blocks_per_seq)
    _sg_kernel(u_ref, sv_ref, lng_ref, lnb_ref, ws_ref, bs_ref, ysg_ref, chunks_per_step=MIX_BLOCKS)


def _mixers(proj, layer, seq_len, q_gain, k_gain, sinks, sg_ln_g, sg_ln_b, sg_w, sg_b,
            ssm_ops, d_skip, w_glu, b_glu):
    n = proj.shape[0]
    tm = MIX_BLOCKS * BLK
    nj = tm // SSM_SUB
    toep, w2, cre, cim, a8 = ssm_ops
    slopes = 2.0 ** (-8.0 * jnp.arange(1, N_Q_HEADS + 1, dtype=F32) / N_Q_HEADS)
    ri = jnp.arange(BLK)[:, None]
    ci = jnp.arange(BLK)[None, :]
    dist = jnp.where(ci <= ri, ri - ci, ri - ci + BLK).astype(F32)
    bias = (slopes[:, None, None] * dist).reshape(N_KV_HEADS, Q_PER_KV * BLK, BLK)
    sink_tile = jnp.broadcast_to(sinks.astype(F32)[:, None, None], (N_Q_HEADS, BLK, BLK))
    sink_tile = sink_tile.reshape(N_KV_HEADS, Q_PER_KV * BLK, BLK)

    def cols(width, off):
        return pl.BlockSpec((tm, width), lambda i: (i, off // width))

    def prev_block(off):
        return pl.BlockSpec((BLK, KV_W), lambda i: (jnp.maximum(i * MIX_BLOCKS - 1, 0), off // KV_W))

    def const(arr):
        return pl.BlockSpec(arr.shape, lambda i: (0,) * arr.ndim)

    def layer_slab(arr):
        nd = arr.ndim - 1
        return pl.BlockSpec((None,) + arr.shape[1:], lambda i: (layer,) + (0,) * nd,
                            pipeline_mode=pl.Buffered(1))

    k_off, v_off = ATTN_W, ATTN_W + KV_W
    u_off, sv_off, x_off = ATTN_W + 2 * KV_W, ATTN_W + 2 * KV_W + SG_W, MAIN_W - SSM_W
    attn_consts = (q_gain.reshape(1, HEAD_DIM).astype(F32), k_gain.reshape(1, HEAD_DIM).astype(F32),
                   bias, sink_tile)
    sg_consts = (sg_ln_g.reshape(1, SG_W).astype(F32), sg_ln_b.reshape(1, SG_W).astype(F32),
                 sg_w.astype(F32), sg_b.astype(F32)[..., None])
    ssm_consts = (toep, w2, cre, cim, a8, d_skip, w_glu, b_glu)
    row = lambda w: pl.BlockSpec((tm, w), lambda i: (i, 0))
    return pl.pallas_call(
        functools.partial(_mixers_kernel, blocks_per_seq=seq_len // BLK),
        grid=(n // tm,),
        in_specs=[cols(ATTN_W, 0), cols(KV_W, k_off), cols(KV_W, v_off), prev_block(k_off), prev_block(v_off)]
                 + [const(a) for a in attn_consts]
                 + [cols(SG_W, u_off), cols(SG_W, sv_off)] + [const(a) for a in sg_consts]
                 + [cols(SSM_W, x_off)] + [layer_slab(a) for a in ssm_consts],
        out_specs=[row(ATTN_W), row(SG_W), row(SSM_W)],
        out_shape=[jax.ShapeDtypeStruct((n, w), BF16) for w in (ATTN_W, SG_W, SSM_W)],
        scratch_shapes=[pltpu.VMEM((SSM_KT, tm, BLK), F32),
                        pltpu.VMEM((nj, 2 * SSM_NSTATE), F32),
                        pltpu.VMEM((2, SSM_NSTATE), F32)],
        compiler_params=_cparams(("arbitrary",)),
        name="mixers",
    )(proj, proj, proj, proj, proj, *attn_consts, proj, proj, *sg_consts, proj, *ssm_consts)


def _merge_kernel(x_ref, g0a_ref, g0b_ref, g1a_ref, g1b_ref, g2a_ref, g2b_ref, ya_ref, ys_ref, ym_ref,
                  bg_ref, wa_ref, ws_ref, wm_ref, wo_ref, n2_ref, xo_ref, h_ref):
    d = D_MODEL

    def gate(lo_ref, hi_ref, b):
        logits = jnp.concatenate([lo_ref[...], hi_ref[...]], axis=1).astype(F32)
        return _sigmoid(logits + bg_ref[:, b * d:(b + 1) * d])

    merged = gate(g0a_ref, g0b_ref, 0) * _dot(ya_ref[...], wa_ref[...])
    merged = merged + gate(g1a_ref, g1b_ref, 1) * _dot(ys_ref[...], ws_ref[...])
    merged = merged + gate(g2a_ref, g2b_ref, 2) * _dot(ym_ref[...], wm_ref[...])
    xn = x_ref[...] + _dot(merged.astype(BF16), wo_ref[...])
    xo_ref[...] = xn
    h_ref[...] = (xn * _row_inv_rms(xn) * n2_ref[...]).astype(h_ref.dtype)


def _merge(x, proj, y_attn, y_sg, y_ssm, b_gate, wa, ws, wm, wo, norm2_g, layer, tm=256):
    n, d = x.shape
    half = d // 2
    g_col = MAIN_W // half
    row = lambda w: pl.BlockSpec((tm, w), lambda i: (i, 0))
    gate = lambda c: pl.BlockSpec((tm, half), lambda i: (i, g_col + c))

    def res(arr):
        nd = arr.ndim - 1
        return pl.BlockSpec((None,) + arr.shape[1:], lambda i: (layer,) + (0,) * nd,
                            pipeline_mode=pl.Buffered(1))

    return pl.pallas_call(
        _merge_kernel,
        grid=(n // tm,),
        in_specs=[row(d)] + [gate(c) for c in range(6)]
                 + [row(ATTN_W), row(SG_W), row(SSM_W),
                    res(b_gate), res(wa), res(ws), res(wm), res(wo), res(norm2_g)],
        out_specs=[row(d), row(d)],
        out_shape=[jax.ShapeDtypeStruct((n, d), F32), jax.ShapeDtypeStruct((n, d), BF16)],
        compiler_params=_cparams(("parallel",)),
        name="merge_outproj",
    )(x, proj, proj, proj, proj, proj, proj, y_attn, y_sg, y_ssm, b_gate, wa, ws, wm, wo, norm2_g)


def _ffn_up_kernel(h_ref, wg_ref, wv_ref, cwg_ref, cwv_ref, cbg_ref, cbv_ref, o_ref,
                   wb_ref, e_ref, *, tiles_per_seq, row_chunks):
    i = pl.program_id(1)
    tm = h_ref.shape[0]
    tn = o_ref.shape[1]
    tc = tm // row_chunks
    halo = F32_SUBLANES
    nslab = tn // BLK

    @pl.when(i == 0)
    def _():
        wb_ref[:, :tn] = wg_ref[...].astype(BF16)
        wb_ref[:, tn:] = wv_ref[...].astype(BF16)

    @pl.when(i % tiles_per_seq == 0)
    def _():
        e_ref[:, 0:halo, :] = jnp.zeros((2 * nslab, halo, BLK), F32)

    def conv(slab, base, cw_ref, cb_ref, lanes):
        return (cb_ref[:, lanes] + cw_ref[0:1, lanes] * e_ref[slab, base:base + tc, :]
                + cw_ref[1:2, lanes] * e_ref[slab, base - 1:base - 1 + tc, :]
                + cw_ref[2:3, lanes] * e_ref[slab, base - 2:base - 2 + tc, :])

    for r in range(row_chunks):
        rows = slice(r * tc, (r + 1) * tc)
        base = halo + r * tc
        e = _dot(h_ref[rows, :], wb_ref[...])
        for s in range(2 * nslab):
            e_ref[s, base:base + tc, :] = e[:, s * BLK:(s + 1) * BLK]
        for s in range(nslab):
            lanes = slice(s * BLK, (s + 1) * BLK)
            gate = conv(s, base, cwg_ref, cbg_ref, lanes)
            val = conv(nslab + s, base, cwv_ref, cbv_ref, lanes)
            o_ref[rows, lanes] = (_gelu(gate) * val).astype(o_ref.dtype)
    e_ref[:, 0:halo, :] = e_ref[:, tm:tm + halo, :]


def _ffn_up(h, w_up, conv_w, conv_b, layer, seq_len, tm=2048, tn=512, row_chunks=4):
    n, d = h.shape
    nj = D_FF // tn
    wspec = lambda off: pl.BlockSpec((None, d, tn), lambda j, i: (layer, 0, j + off))
    cspec = lambda rows, off: pl.BlockSpec((None, rows, tn), lambda j, i: (layer, 0, j + off))
    return pl.pallas_call(
        functools.partial(_ffn_up_kernel, tiles_per_seq=seq_len // tm, row_chunks=row_chunks),
        grid=(nj, n // tm),
        in_specs=[pl.BlockSpec((tm, d), lambda j, i: (i, 0)),
                  wspec(0), wspec(nj), cspec(3, 0), cspec(3, nj), cspec(1, 0), cspec(1, nj)],
        out_specs=pl.BlockSpec((tm, tn), lambda j, i: (i, j)),
        out_shape=jax.ShapeDtypeStruct((n, D_FF), BF16),
        scratch_shapes=[pltpu.VMEM((d, 2 * tn), BF16),
                        pltpu.VMEM((2 * tn // BLK, F32_SUBLANES + tm, BLK), F32)],
        compiler_params=_cparams(("parallel", "arbitrary")),
        name="ffn_up_conv",
    )(h, w_up, w_up, conv_w, conv_w, conv_b, conv_b)


def _ffn_down_kernel(a_ref, w_ref, x_ref, xo_ref):
    xo_ref[...] = x_ref[...] + _dot(a_ref[...], w_ref[...])


def _ffn_down_norm_kernel(a_ref, w_ref, x_ref, g_ref, xo_ref, h_ref, inv_ref, ssq_ref):
    j = pl.program_id(1)
    xn = x_ref[...] + _dot(a_ref[...], w_ref[...])
    xo_ref[...] = xn
    h_ref[...] = (xn * g_ref[...]).astype(h_ref.dtype)

    @pl.when(j == 0)
    def _():
        ssq_ref[...] = jnp.zeros_like(ssq_ref)

    sq = xn * xn
    part = sq[:, 0:BLK]
    for c in range(1, sq.shape[1] // BLK):
        part = part + sq[:, c * BLK:(c + 1) * BLK]
    ssq_ref[...] += part

    @pl.when(j == pl.num_programs(1) - 1)
    def _():
        ssq = jnp.sum(ssq_ref[...], axis=-1, keepdims=True)
        inv_ref[...] = lax.rsqrt(ssq * (1.0 / D_MODEL) + EPS)


FFN_DOWN_TN = 512


def _cast_kernel(w_ref, o_ref):
    o_ref[...] = w_ref[...].astype(o_ref.dtype)


def _ffn_down_weights(w_down, row_blocks=4):
    nl, k, d = w_down.shape
    tn = FFN_DOWN_TN
    tk = k // row_blocks
    return pl.pallas_call(
        _cast_kernel,
        grid=(nl, d // tn, row_blocks),
        in_specs=[pl.BlockSpec((None, tk, tn), lambda l, j, r: (l, r, j))],
        out_specs=pl.BlockSpec((None, None, tk, tn), lambda l, j, r: (l, j, r, 0)),
        out_shape=jax.ShapeDtypeStruct((nl, d // tn, k, tn), BF16),
        compiler_params=_cparams(("parallel", "parallel", "parallel")),
        name="ffn_down_weight_tiles",
    )(w_down)


def _ffn_down(act, w_down, x, norm_g, layer, next_layer, tm=1024):
    n, d = x.shape
    k = act.shape[1]
    tn = FFN_DOWN_TN
    tile = pl.BlockSpec((tm, tn), lambda i, j: (i, j))
    in_specs = [pl.BlockSpec((tm, k), lambda i, j: (i, 0)),
                pl.BlockSpec((None, None, k, tn), lambda i, j: (layer, j, 0, 0)),
                tile]
    common = dict(grid=(n // tm, d // tn), compiler_params=_cparams(("parallel", "arbitrary")))
    x_shape = jax.ShapeDtypeStruct((n, d), F32)
    if next_layer is None:
        xo = pl.pallas_call(_ffn_down_kernel, in_specs=in_specs, out_specs=tile, out_shape=x_shape,
                            name="ffn_down_last", **common)(act, w_down, x)
        return xo, None, None
    return pl.pallas_call(
        _ffn_down_norm_kernel,
        in_specs=in_specs + [pl.BlockSpec((None, 1, tn), lambda i, j: (next_layer, 0, j))],
        out_specs=[tile, tile, pl.BlockSpec((tm, 1), lambda i, j: (i, 0))],
        out_shape=[x_shape, jax.ShapeDtypeStruct((n, d), BF16), jax.ShapeDtypeStruct((n, 1), F32)],
        scratch_shapes=[pltpu.VMEM((tm, BLK), F32)],
        name="ffn_down",
        **common,
    )(act, w_down, x, norm_g)


def kernel(x, norm1_g, w_in, b_gate, q_norm_g, k_norm_g, attn_sinks, sg_ln_g, sg_ln_b, sg_w, sg_b, ssm_a_re, ssm_a_im, ssm_log_dt, ssm_b_re, ssm_b_im, ssm_c_re, ssm_c_im, ssm_d, ssm_w_glu, ssm_b_glu, w_proj_attn, w_proj_sg, w_proj_ssm, w_out, norm2_g, ffn_w_up, ffn_conv_w, ffn_conv_b, ffn_w_down):
    b, seq_len, d = x.shape
    depth = w_in.shape[0]
    xr = x.reshape(b * seq_len, d)
    ssm_ops = _ssm_operators(ssm_a_re, ssm_a_im, ssm_log_dt, ssm_b_re, ssm_b_im, ssm_c_re, ssm_c_im)
    w_glu, wa, ws, wm, wo = (w.astype(BF16) for w in (
        ssm_w_glu, w_proj_attn, w_proj_sg, w_proj_ssm, w_out))
    w_down = _ffn_down_weights(ffn_w_down)
    rowvec = lambda v: v.astype(F32).reshape(depth, 1, -1)
    ssm_d_r, b_glu_r, b_gate_r, norm1_r, norm2_r, conv_b_r = (rowvec(v) for v in (
        ssm_d, ssm_b_glu, b_gate, norm1_g, norm2_g, ffn_conv_b))

    h, inv = _rmsnorm(xr, norm1_r, 0)
    for l in range(depth):
        proj = _in_proj(h, inv, w_in, l)
        y_attn, y_sg, y_ssm = _mixers(proj, l, seq_len, q_norm_g[l], k_norm_g[l], attn_sinks[l],
                                      sg_ln_g[l], sg_ln_b[l], sg_w[l], sg_b[l],
                                      ssm_ops, ssm_d_r, w_glu, b_glu_r)
        xr, h2 = _merge(xr, proj, y_attn, y_sg, y_ssm, b_gate_r, wa, ws, wm, wo, norm2_r, l)
        act = _ffn_up(h2, ffn_w_up, ffn_conv_w, conv_b_r, l, seq_len)
        xr, h, inv = _ffn_down(act, w_down, xr, norm1_r, l, l + 1 if l + 1 < depth else None)
    return xr.reshape(b, seq_len, d)
```

```python
import functools
import math

import jax
import jax.numpy as jnp
from jax import lax
from jax.experimental import pallas as pl
from jax.experimental.pallas import tpu as pltpu

F32 = jnp.float32
BF16 = jnp.bfloat16

D_MODEL = 2048
N_Q_HEADS = 8
N_KV_HEADS = 2
Q_PER_KV = N_Q_HEADS // N_KV_HEADS
HEAD_DIM = 128
BLK = 128
ATTN_W = N_Q_HEADS * HEAD_DIM
KV_W = N_KV_HEADS * HEAD_DIM
SG_GROUPS = 4
SG_W = 512
SSM_GROUPS = 32
SSM_CH = 16
SSM_W = 512
SSM_STATE = 64
SSM_SUB = 8
SSM_KT = 4
SSM_NSTATE = SSM_GROUPS * SSM_STATE
MAIN_W = ATTN_W + 2 * KV_W + 2 * SG_W + SSM_W
GATE_W = 3 * D_MODEL
IN_W = MAIN_W + GATE_W
D_FF = 5632
F32_SUBLANES = 8
EPS = 1e-6

VMEM_LIMIT = 56 * 1024 * 1024
MERGE_VMEM_LIMIT = 58 * 1024 * 1024


def _cparams(sem, vmem_limit=VMEM_LIMIT):
    return pltpu.CompilerParams(dimension_semantics=sem, vmem_limit_bytes=vmem_limit)


def _gelu(x):
    c = math.sqrt(2.0 / math.pi)
    return 0.5 * x * (1.0 + jnp.tanh(c * (x + 0.044715 * (x * x * x))))


def _sigmoid(x):
    return 1.0 / (1.0 + jnp.exp(-x))


def _dot(a, b):
    return jnp.dot(a, b, preferred_element_type=F32)


def _row_inv_rms(x):
    return lax.rsqrt(jnp.mean(x * x, axis=-1, keepdims=True) + EPS)


def _rmsnorm_kernel(x_ref, g_ref, o_ref, inv_ref):
    x = x_ref[...]
    o_ref[...] = (x * g_ref[...]).astype(o_ref.dtype)
    inv_ref[...] = _row_inv_rms(x)


def _rmsnorm(x, g, layer, tm=1024):
    n, d = x.shape
    return pl.pallas_call(
        _rmsnorm_kernel,
        grid=(n // tm,),
        in_specs=[pl.BlockSpec((tm, d), lambda i: (i, 0)),
                  pl.BlockSpec((None, 1, d), lambda i: (layer, 0, 0))],
        out_specs=[pl.BlockSpec((tm, d), lambda i: (i, 0)), pl.BlockSpec((tm, 1), lambda i: (i, 0))],
        out_shape=[jax.ShapeDtypeStruct((n, d), BF16), jax.ShapeDtypeStruct((n, 1), F32)],
        compiler_params=_cparams(("parallel",)),
        name="rmsnorm",
    )(x, g)


def _in_proj_kernel(h_ref, inv_ref, w_ref, o_ref, wb_ref):
    @pl.when(pl.program_id(1) == 0)
    def _():
        wb_ref[...] = w_ref[...].astype(BF16)

    o_ref[...] = (_dot(h_ref[...], wb_ref[...]) * inv_ref[...]).astype(o_ref.dtype)


def _in_proj(h, inv, w_in, layer, tm=1024, tn=1024):
    n, d = h.shape
    width = w_in.shape[2]
    nj = width // tn
    main_tiles = MAIN_W // tn
    return pl.pallas_call(
        _in_proj_kernel,
        grid=(nj, n // tm),
        in_specs=[pl.BlockSpec((tm, d), lambda j, i: (i, 0)),
                  pl.BlockSpec((tm, 1), lambda j, i: (i, 0)),
                  pl.BlockSpec((None, d, tn), lambda j, i: (layer, 0, (j + main_tiles) % nj))],
        out_specs=pl.BlockSpec((tm, tn), lambda j, i: (i, j)),
        out_shape=jax.ShapeDtypeStruct((n, width), BF16),
        scratch_shapes=[pltpu.VMEM((d, tn), BF16)],
        compiler_params=_cparams(("parallel", "arbitrary")),
        name="in_proj",
    )(h, inv, w_in)


def _attn_kernel(q_ref, k_ref, v_ref, kp_ref, vp_ref, qg_ref, kg_ref, bias_ref, sink_ref, o_ref,
                 *, blocks_per_step, blocks_per_seq):
    i = pl.program_id(0)
    qg = qg_ref[...] * (HEAD_DIM ** -0.5)
    kg = kg_ref[...]
    row = lax.broadcasted_iota(jnp.int32, (BLK, BLK), 0)
    col = lax.broadcasted_iota(jnp.int32, (BLK, BLK), 1)
    cur4 = jnp.concatenate([col <= row] * Q_PER_KV, axis=0)
    ones = jnp.ones((BLK, BLK), BF16)
    ones_win = jnp.ones((2 * BLK, BLK), BF16)

    def rms(x, g):
        ssq = _dot((x * x).astype(BF16), ones)
        return x * lax.rsqrt(ssq * (1.0 / HEAD_DIM) + EPS) * g

    k_prev = None
    for r in range(blocks_per_step):
        rows = slice(r * BLK, (r + 1) * BLK)
        if r == 0:
            has_prev = (i * blocks_per_step) % blocks_per_seq != 0
            k_prev = [rms(kp_ref[:, h * HEAD_DIM:(h + 1) * HEAD_DIM].astype(F32), kg)
                      for h in range(N_KV_HEADS)]
            v_prev = vp_ref[...]
        else:
            has_prev = None
            v_prev = v_ref[(r - 1) * BLK:r * BLK, :]
        k_cur = []
        for h in range(N_KV_HEADS):
            hc = slice(h * HEAD_DIM, (h + 1) * HEAD_DIM)
            k_cur.append(rms(k_ref[rows, hc].astype(F32), kg))
            kwin = jnp.concatenate([k_prev[h], k_cur[h]], axis=0).astype(BF16)
            vwin = jnp.concatenate([v_prev[:, hc], v_ref[rows, hc]], axis=0)
            vext = jnp.concatenate([vwin, ones_win], axis=1)
            q4 = jnp.concatenate(
                [q_ref[rows, (h * Q_PER_KV + g) * HEAD_DIM:(h * Q_PER_KV + g + 1) * HEAD_DIM].astype(F32)
                 for g in range(Q_PER_KV)], axis=0)
            q4 = rms(q4, qg).astype(BF16)
            s2 = lax.dot_general(q4, kwin, (((1,), (1,)), ((), ())),
                                 preferred_element_type=F32)
            s = jnp.where(cur4, s2[:, BLK:], s2[:, :BLK]) - bias_ref[h]
            if has_prev is not None:
                s = jnp.where(jnp.logical_or(cur4, has_prev), s, -jnp.inf)
            sink = sink_ref[h]
            m = jnp.maximum(jnp.max(s, axis=-1, keepdims=True), sink)
            p = jnp.exp(s - m)
            p2 = jnp.concatenate([jnp.where(cur4, 0.0, p), jnp.where(cur4, p, 0.0)],
                                 axis=1).astype(BF16)
            ov = _dot(p2, vext)
            o = ov[:, :BLK] / (ov[:, BLK:] + jnp.exp(sink - m))
            for g in range(Q_PER_KV):
                c0 = (h * Q_PER_KV + g) * HEAD_DIM
                o_ref[rows, c0:c0 + HEAD_DIM] = o[g * BLK:(g + 1) * BLK].astype(o_ref.dtype)
        k_prev = k_cur


def _sg_kernel(u_ref, v_ref, lng_ref, lnb_ref, w_ref, b_ref, o_ref, *, chunks_per_step):
    row = lax.broadcasted_iota(jnp.int32, (BLK, BLK), 0)
    col = lax.broadcasted_iota(jnp.int32, (BLK, BLK), 1)
    tril = col <= row
    for c in range(chunks_per_step):
        rows = slice(c * BLK, (c + 1) * BLK)
        for g in range(SG_GROUPS):
            gc = slice(g * BLK, (g + 1) * BLK)
            zv = _gelu(v_ref[rows, gc].astype(F32))
            mu = jnp.mean(zv, axis=-1, keepdims=True)
            zc = zv - mu
            var = jnp.mean(zc * zc, axis=-1, keepdims=True)
            zn = zc * lax.rsqrt(var + EPS) * lng_ref[:, gc] + lnb_ref[:, gc]
            w = jnp.where(tril, w_ref[g], 0.0).astype(BF16)
            mixed = _dot(w, zn.astype(BF16)) + b_ref[g]
            zu = _gelu(u_ref[rows, gc].astype(F32))
            o_ref[rows, gc] = (zu * mixed).astype(o_ref.dtype)


def _ssm_operators(a_re, a_im, log_dt, b_re, b_im, c_re, c_im):
    hp = lax.Precision.HIGHEST
    p, c, kt, sub = SSM_STATE, SSM_CH, SSM_KT, SSM_SUB
    gl = SSM_GROUPS // kt
    nl = a_re.shape[0]
    a_re, a_im = a_re.astype(F32)[:, None], a_im.astype(F32)[:, None]
    dt = jnp.exp(log_dt.astype(F32))[:, None, :, None]
    k = jnp.arange(sub + 1, dtype=F32)[None, :, None, None]
    mag = jnp.exp(a_re * dt * k)
    pr, pi = mag * jnp.cos(a_im * dt * k), mag * jnp.sin(a_im * dt * k)
    ar, ai, lr, li = pr[:, 1], pi[:, 1], a_re[:, 0], a_im[:, 0]
    den = lr * lr + li * li
    cr = ((ar - 1.0) * lr + ai * li) / den
    ci = (ai * lr - (ar - 1.0) * li) / den
    bbr = cr[..., None] * b_re - ci[..., None] * b_im
    bbi = cr[..., None] * b_im + ci[..., None] * b_re
    bbr_t, bbi_t = jnp.swapaxes(bbr, -1, -2), jnp.swapaxes(bbi, -1, -2)
    car = c_re[:, None] * pr[:, :, :, None, :] - c_im[:, None] * pi[:, :, :, None, :]
    cai = c_re[:, None] * pi[:, :, :, None, :] + c_im[:, None] * pr[:, :, :, None, :]

    def expand(x, rows_per_group, blocks, width):
        col = jnp.arange(blocks * gl * width)
        src = (col // (gl * width)) * width + col % width
        spread = (jnp.arange(blocks * width)[:, None] == src[None, :]).astype(BF16)
        row_group = (jnp.arange(x.shape[2]) // rows_per_group) % gl
        mask = (row_group[:, None] == ((col // width) % gl)[None, :]).astype(BF16)
        return jnp.einsum("xkrm,mn->xkrn", x.astype(BF16), spread, preferred_element_type=BF16) * mask

    kk = (jnp.einsum("xgdp,xlgcp->xlgdc", bbr_t, car[:, :sub], precision=hp)
          - jnp.einsum("xgdp,xlgcp->xlgdc", bbi_t, cai[:, :sub], precision=hp))
    krow = kk.reshape(nl, sub, kt, gl * c, c).transpose(0, 2, 3, 1, 4).reshape(nl, kt, gl * c, sub * c)
    krow = jnp.pad(krow, ((0, 0), (0, 0), (0, 0), (sub * c, 0)))
    kst = jnp.concatenate([krow[..., (sub - s) * c:(2 * sub - s) * c] for s in range(sub)], axis=2)
    toep = expand(kst, c, sub, c)

    rr, ri = pr[:, sub - 1::-1][:, :, :, None, :], pi[:, sub - 1::-1][:, :, :, None, :]
    wr = rr * bbr_t[:, None] - ri * bbi_t[:, None]
    wi = rr * bbi_t[:, None] + ri * bbr_t[:, None]

    def inject(x):
        x = x.reshape(nl, sub, kt, gl * c, p).transpose(0, 2, 1, 3, 4)
        return expand(x.reshape(nl, kt, sub * gl * c, p), c, 1, p)

    w2 = jnp.concatenate([inject(wr), inject(wi)], axis=-1)

    def readout(x):
        x = x.reshape(nl, sub, kt, gl, c, p).transpose(0, 2, 3, 5, 1, 4)
        return expand(x.reshape(nl, kt, gl * p, sub * c), p, sub, c)

    cre = readout(car[:, 1:])
    cim = readout(-cai[:, 1:])
    a8 = jnp.stack([pr[:, sub].reshape(nl, -1), pi[:, sub].reshape(nl, -1)], axis=1)
    return toep, w2, cre, cim, a8


def _ssm_kernel(x_ref, toep_ref, w2_ref, cre_ref, cim_ref, a8_ref, d_ref, wg_ref, bg_ref, o_ref,
                slab_ref, st_ref, carry_ref, *, tiles_per_seq):
    i = pl.program_id(0)
    tm = x_ref.shape[0]
    nj = tm // SSM_SUB
    ns = SSM_NSTATE
    gw = ns // SSM_KT

    @pl.when(i % tiles_per_seq == 0)
    def _():
        carry_ref[...] = jnp.zeros_like(carry_ref)

    xall = []
    for kt in range(SSM_KT):
        slab_ref[kt] = x_ref[:, kt * BLK:(kt + 1) * BLK].astype(F32)
        xall.append(jnp.concatenate(
            [slab_ref[kt, pl.ds(s, nj, stride=SSM_SUB), :].astype(BF16) for s in range(SSM_SUB)], axis=1))

    for kt in range(SSM_KT):
        acc = _dot(xall[kt], w2_ref[kt])
        st_ref[:, kt * gw:(kt + 1) * gw] = acc[:, :gw]
        st_ref[:, ns + kt * gw:ns + (kt + 1) * gw] = acc[:, gw:]

    a8r = a8_ref[0:1, :]
    a8i = a8_ref[1:2, :]

    def step(j, st):
        sr, si = st
        lr = st_ref[pl.ds(j, 1), :ns]
        li = st_ref[pl.ds(j, 1), ns:]
        st_ref[pl.ds(j, 1), :ns] = sr
        st_ref[pl.ds(j, 1), ns:] = si
        return (a8r * sr - a8i * si + lr, a8r * si + a8i * sr + li)

    sr, si = lax.fori_loop(0, nj, step, (carry_ref[0:1, :], carry_ref[1:2, :]), unroll=8)
    carry_ref[0:1, :] = sr
    carry_ref[1:2, :] = si

    stb = st_ref[...].astype(BF16)
    for kt in range(SSM_KT):
        y = (_dot(xall[kt], toep_ref[kt]) + _dot(stb[:, kt * gw:(kt + 1) * gw], cre_ref[kt])
             + _dot(stb[:, ns + kt * gw:ns + (kt + 1) * gw], cim_ref[kt]))
        for t in range(SSM_SUB):
            slab_ref[kt, pl.ds(t, nj, stride=SSM_SUB), :] = y[:, t * BLK:(t + 1) * BLK]
    y = jnp.concatenate([slab_ref[kt] for kt in range(SSM_KT)], axis=1)
    y = _gelu(y + d_ref[...] * x_ref[...].astype(F32))
    o_ref[...] = (y * _sigmoid(_dot(y.astype(BF16), wg_ref[...]) + bg_ref[...])).astype(o_ref.dtype)


MIX_BLOCKS = 8


def _mixers_kernel(q_ref, k_ref, v_ref, kp_ref, vp_ref, qg_ref, kg_ref, bias_ref, sink_ref,
                   u_ref, sv_ref, lng_ref, lnb_ref, ws_ref, bs_ref,
                   x_ref, toep_ref, w2_ref, cre_ref, cim_ref, a8_ref, d_ref, wg_ref, bg_ref,
                   ya_ref, ysg_ref, yssm_ref, slab_ref, st_ref, carry_ref, *, blocks_per_seq):
    _ssm_kernel(x_ref, toep_ref, w2_ref, cre_ref, cim_ref, a8_ref, d_ref, wg_ref, bg_ref, yssm_ref,
                slab_ref, st_ref, carry_ref, tiles_per_seq=blocks_per_seq // MIX_BLOCKS)
    _attn_kernel(q_ref, k_ref, v_ref, kp_ref, vp_ref, qg_ref, kg_ref, bias_ref, sink_ref, ya_ref,
                 blocks_per_step=MIX_BLOCKS, blocks_per_seq=blocks_per_seq)
    _sg_kernel(u_ref, sv_ref, lng_ref, lnb_ref, ws_ref, bs_ref, ysg_ref, chunks_per_step=MIX_BLOCKS)


def _mixers(proj, layer, seq_len, q_gain, k_gain, sinks, sg_ln_g, sg_ln_b, sg_w, sg_b,
            ssm_ops, d_skip, w_glu, b_glu):
    n = proj.shape[0]
    tm = MIX_BLOCKS * BLK
    nj = tm // SSM_SUB
    toep, w2, cre, cim, a8 = ssm_ops
    slopes = 2.0 ** (-8.0 * jnp.arange(1, N_Q_HEADS + 1, dtype=F32) / N_Q_HEADS)
    ri = jnp.arange(BLK)[:, None]
    ci = jnp.arange(BLK)[None, :]
    dist = jnp.where(ci <= ri, ri - ci, ri - ci + BLK).astype(F32)
    bias = (slopes[:, None, None] * dist).reshape(N_KV_HEADS, Q_PER_KV * BLK, BLK)
    sink_tile = jnp.broadcast_to(sinks.astype(F32)[:, None, None], (N_Q_HEADS, BLK, BLK))
    sink_tile = sink_tile.reshape(N_KV_HEADS, Q_PER_KV * BLK, BLK)

    def cols(width, off):
        return pl.BlockSpec((tm, width), lambda i: (i, off // width))

    def prev_block(off):
        return pl.BlockSpec((BLK, KV_W), lambda i: (jnp.maximum(i * MIX_BLOCKS - 1, 0), off // KV_W))

    def const(arr):
        return pl.BlockSpec(arr.shape, lambda i: (0,) * arr.ndim)

    def layer_slab(arr):
        nd = arr.ndim - 1
        return pl.BlockSpec((None,) + arr.shape[1:], lambda i: (layer,) + (0,) * nd,
                            pipeline_mode=pl.Buffered(1))

    q_off = GATE_W
    k_off, v_off = q_off + ATTN_W, q_off + ATTN_W + KV_W
    u_off, sv_off, x_off = v_off + KV_W, v_off + KV_W + SG_W, v_off + KV_W + 2 * SG_W
    attn_consts = (q_gain.reshape(1, HEAD_DIM).astype(F32), k_gain.reshape(1, HEAD_DIM).astype(F32),
                   bias, sink_tile)
    sg_consts = (sg_ln_g.reshape(1, SG_W).astype(F32), sg_ln_b.reshape(1, SG_W).astype(F32),
                 sg_w.astype(F32), sg_b.astype(F32)[..., None])
    ssm_consts = (toep, w2, cre, cim, a8, d_skip, w_glu, b_glu)
    row = lambda w: pl.BlockSpec((tm, w), lambda i: (i, 0))
    return pl.pallas_call(
        functools.partial(_mixers_kernel, blocks_per_seq=seq_len // BLK),
        grid=(n // tm,),
        in_specs=[cols(ATTN_W, q_off), cols(KV_W, k_off), cols(KV_W, v_off), prev_block(k_off), prev_block(v_off)]
                 + [const(a) for a in attn_consts]
                 + [cols(SG_W, u_off), cols(SG_W, sv_off)] + [const(a) for a in sg_consts]
                 + [cols(SSM_W, x_off)] + [layer_slab(a) for a in ssm_consts],
        out_specs=[row(ATTN_W), row(SG_W), row(SSM_W)],
        out_shape=[jax.ShapeDtypeStruct((n, w), BF16) for w in (ATTN_W, SG_W, SSM_W)],
        scratch_shapes=[pltpu.VMEM((SSM_KT, tm, BLK), F32),
                        pltpu.VMEM((nj, 2 * SSM_NSTATE), F32),
                        pltpu.VMEM((2, SSM_NSTATE), F32)],
        compiler_params=_cparams(("arbitrary",)),
        name="mixers",
    )(proj, proj, proj, proj, proj, *attn_consts, proj, proj, *sg_consts, proj, *ssm_consts)


def _merge_kernel(x_ref, g0_ref, g1_ref, g2_ref, ya_ref, ys_ref, ym_ref,
                  bg_ref, wa_ref, ws_ref, wm_ref, wo_ref, n2_ref, xo_ref, h_ref):
    d = D_MODEL

    def gate(g_ref, b):
        return _sigmoid(g_ref[...].astype(F32) + bg_ref[:, b * d:(b + 1) * d])

    merged = gate(g0_ref, 0) * _dot(ya_ref[...], wa_ref[...])
    merged = merged + gate(g1_ref, 1) * _dot(ys_ref[...], ws_ref[...])
    merged = merged + gate(g2_ref, 2) * _dot(ym_ref[...], wm_ref[...])
    xn = x_ref[...] + _dot(merged.astype(BF16), wo_ref[...])
    xo_ref[...] = xn
    h_ref[...] = (xn * _row_inv_rms(xn) * n2_ref[...]).astype(h_ref.dtype)


def _merge(x, proj, y_attn, y_sg, y_ssm, b_gate, wa, ws, wm, wo, norm2_g, layer, tm=512):
    n, d = x.shape
    row = lambda w: pl.BlockSpec((tm, w), lambda i: (i, 0))
    gate = lambda b: pl.BlockSpec((tm, d), lambda i: (i, b))

    def res(arr):
        nd = arr.ndim - 1
        return pl.BlockSpec((None,) + arr.shape[1:], lambda i: (layer,) + (0,) * nd,
                            pipeline_mode=pl.Buffered(1))

    return pl.pallas_call(
        _merge_kernel,
        grid=(n // tm,),
        in_specs=[row(d)] + [gate(b) for b in range(3)]
                 + [row(ATTN_W), row(SG_W), row(SSM_W),
                    res(b_gate), res(wa), res(ws), res(wm), res(wo), res(norm2_g)],
        out_specs=[row(d), row(d)],
        out_shape=[jax.ShapeDtypeStruct((n, d), F32), jax.ShapeDtypeStruct((n, d), BF16)],
        compiler_params=_cparams(("parallel",), MERGE_VMEM_LIMIT),
        name="merge_outproj",
    )(x, proj, proj, proj, y_attn, y_sg, y_ssm, b_gate, wa, ws, wm, wo, norm2_g)


def _ffn_up_kernel(h_ref, wg_ref, wv_ref, cwg_ref, cwv_ref, cbg_ref, cbv_ref, o_ref,
                   wb_ref, e_ref, *, tiles_per_seq, row_chunks):
    i = pl.program_id(1)
    tm = h_ref.shape[0]
    tn = o_ref.shape[1]
    tc = tm // row_chunks
    halo = F32_SUBLANES
    nslab = tn // BLK

    @pl.when(i == 0)
    def _():
        wb_ref[:, :tn] = wg_ref[...].astype(BF16)
        wb_ref[:, tn:] = wv_ref[...].astype(BF16)

    @pl.when(i % tiles_per_seq == 0)
    def _():
        e_ref[:, 0:halo, :] = jnp.zeros((2 * nslab, halo, BLK), F32)

    def conv(slab, base, cw_ref, cb_ref, lanes):
        return (cb_ref[:, lanes] + cw_ref[0:1, lanes] * e_ref[slab, base:base + tc, :]
                + cw_ref[1:2, lanes] * e_ref[slab, base - 1:base - 1 + tc, :]
                + cw_ref[2:3, lanes] * e_ref[slab, base - 2:base - 2 + tc, :])

    for r in range(row_chunks):
        rows = slice(r * tc, (r + 1) * tc)
        base = halo + r * tc
        e = _dot(h_ref[rows, :], wb_ref[...])
        for s in range(2 * nslab):
            e_ref[s, base:base + tc, :] = e[:, s * BLK:(s + 1) * BLK]
        for s in range(nslab):
            lanes = slice(s * BLK, (s + 1) * BLK)
            gate = conv(s, base, cwg_ref, cbg_ref, lanes)
            val = conv(nslab + s, base, cwv_ref, cbv_ref, lanes)
            o_ref[rows, lanes] = (_gelu(gate) * val).astype(o_ref.dtype)
    e_ref[:, 0:halo, :] = e_ref[:, tm:tm + halo, :]


def _ffn_up(h, w_up, conv_w, conv_b, layer, seq_len, tm=2048, tn=512, row_chunks=2):
    n, d = h.shape
    nj = D_FF // tn
    wspec = lambda off: pl.BlockSpec((None, d, tn), lambda j, i: (layer, 0, j + off))
    cspec = lambda rows, off: pl.BlockSpec((None, rows, tn), lambda j, i: (layer, 0, j + off))
    return pl.pallas_call(
        functools.partial(_ffn_up_kernel, tiles_per_seq=seq_len // tm, row_chunks=row_chunks),
        grid=(nj, n // tm),
        in_specs=[pl.BlockSpec((tm, d), lambda j, i: (i, 0)),
                  wspec(0), wspec(nj), cspec(3, 0), cspec(3, nj), cspec(1, 0), cspec(1, nj)],
        out_specs=pl.BlockSpec((tm, tn), lambda j, i: (i, j)),
        out_shape=jax.ShapeDtypeStruct((n, D_FF), BF16),
        scratch_shapes=[pltpu.VMEM((d, 2 * tn), BF16),
                        pltpu.VMEM((2 * tn // BLK, F32_SUBLANES + tm, BLK), F32)],
        compiler_params=_cparams(("parallel", "arbitrary")),
        name="ffn_up_conv",
    )(h, w_up, w_up, conv_w, conv_w, conv_b, conv_b)


def _ffn_down_kernel(a_ref, w_ref, x_ref, xo_ref):
    xo_ref[...] = x_ref[...] + _dot(a_ref[...], w_ref[...])


def _ffn_down_norm_kernel(a_ref, w_ref, x_ref, g_ref, xo_ref, h_ref, inv_ref, ssq_ref):
    j = pl.program_id(1)
    xn = x_ref[...] + _dot(a_ref[...], w_ref[...])
    xo_ref[...] = xn
    h_ref[...] = (xn * g_ref[...]).astype(h_ref.dtype)

    @pl.when(j == 0)
    def _():
        ssq_ref[...] = jnp.zeros_like(ssq_ref)

    sq = xn * xn
    part = sq[:, 0:BLK]
    for c in range(1, sq.shape[1] // BLK):
        part = part + sq[:, c * BLK:(c + 1) * BLK]
    ssq_ref[...] += part

    @pl.when(j == pl.num_programs(1) - 1)
    def _():
        ssq = jnp.sum(ssq_ref[...], axis=-1, keepdims=True)
        inv_ref[...] = lax.rsqrt(ssq * (1.0 / D_MODEL) + EPS)


FFN_DOWN_TN = 512


def _cast_kernel(w_ref, o_ref):
    o_ref[...] = w_ref[...].astype(o_ref.dtype)


def _ffn_down_weights(w_down, row_blocks=4):
    nl, k, d = w_down.shape
    tn = FFN_DOWN_TN
    tk = k // row_blocks
    return pl.pallas_call(
        _cast_kernel,
        grid=(nl, d // tn, row_blocks),
        in_specs=[pl.BlockSpec((None, tk, tn), lambda l, j, r: (l, r, j))],
        out_specs=pl.BlockSpec((None, None, tk, tn), lambda l, j, r: (l, j, r, 0)),
        out_shape=jax.ShapeDtypeStruct((nl, d // tn, k, tn), BF16),
        compiler_params=_cparams(("parallel", "parallel", "parallel")),
        name="ffn_down_weight_tiles",
    )(w_down)


def _ffn_down(act, w_down, x, norm_g, layer, next_layer, tm=1024):
    n, d = x.shape
    k = act.shape[1]
    tn = FFN_DOWN_TN
    tile = pl.BlockSpec((tm, tn), lambda i, j: (i, j))
    in_specs = [pl.BlockSpec((tm, k), lambda i, j: (i, 0)),
                pl.BlockSpec((None, None, k, tn), lambda i, j: (layer, j, 0, 0)),
                tile]
    common = dict(grid=(n // tm, d // tn), compiler_params=_cparams(("parallel", "arbitrary")))
    x_shape = jax.ShapeDtypeStruct((n, d), F32)
    if next_layer is None:
        xo = pl.pallas_call(_ffn_down_kernel, in_specs=in_specs, out_specs=tile, out_shape=x_shape,
                            name="ffn_down_last", **common)(act, w_down, x)
        return xo, None, None
    return pl.pallas_call(
        _ffn_down_norm_kernel,
        in_specs=in_specs + [pl.BlockSpec((None, 1, tn), lambda i, j: (next_layer, 0, j))],
        out_specs=[tile, tile, pl.BlockSpec((tm, 1), lambda i, j: (i, 0))],
        out_shape=[x_shape, jax.ShapeDtypeStruct((n, d), BF16), jax.ShapeDtypeStruct((n, 1), F32)],
        scratch_shapes=[pltpu.VMEM((tm, BLK), F32)],
        name="ffn_down",
        **common,
    )(act, w_down, x, norm_g)


def kernel(x, norm1_g, w_in, b_gate, q_norm_g, k_norm_g, attn_sinks, sg_ln_g, sg_ln_b, sg_w, sg_b, ssm_a_re, ssm_a_im, ssm_log_dt, ssm_b_re, ssm_b_im, ssm_c_re, ssm_c_im, ssm_d, ssm_w_glu, ssm_b_glu, w_proj_attn, w_proj_sg, w_proj_ssm, w_out, norm2_g, ffn_w_up, ffn_conv_w, ffn_conv_b, ffn_w_down):
    b, seq_len, d = x.shape
    depth = w_in.shape[0]
    xr = x.reshape(b * seq_len, d)
    ssm_ops = _ssm_operators(ssm_a_re, ssm_a_im, ssm_log_dt, ssm_b_re, ssm_b_im, ssm_c_re, ssm_c_im)
    w_glu, wa, ws, wm, wo = (w.astype(BF16) for w in (
        ssm_w_glu, w_proj_attn, w_proj_sg, w_proj_ssm, w_out))
    w_down = _ffn_down_weights(ffn_w_down)
    rowvec = lambda v: v.astype(F32).reshape(depth, 1, -1)
    ssm_d_r, b_glu_r, b_gate_r, norm1_r, norm2_r, conv_b_r = (rowvec(v) for v in (
        ssm_d, ssm_b_glu, b_gate, norm1_g, norm2_g, ffn_conv_b))

    h, inv = _rmsnorm(xr, norm1_r, 0)
    for l in range(depth):
        proj = _in_proj(h, inv, w_in, l)
        y_attn, y_sg, y_ssm = _mixers(proj, l, seq_len, q_norm_g[l], k_norm_g[l], attn_sinks[l],
                                      sg_ln_g[l], sg_ln_b[l], sg_w[l], sg_b[l],
                                      ssm_ops, ssm_d_r, w_glu, b_glu_r)
        xr, h2 = _merge(xr, proj, y_attn, y_sg, y_ssm, b_gate_r, wa, ws, wm, wo, norm2_r, l)
        act = _ffn_up(h2, ffn_w_up, ffn_conv_w, conv_b_r, l, seq_len)
        xr, h, inv = _ffn_down(act, w_down, xr, norm1_r, l, l + 1 if l + 1 < depth else None)
    return xr.reshape(b, seq_len, d)
```

```python
import functools
import math

import jax
import jax.numpy as jnp
from jax import lax
from jax.experimental import pallas as pl
from jax.experimental.pallas import tpu as pltpu

F32 = jnp.float32
BF16 = jnp.bfloat16

D_MODEL = 2048
N_Q_HEADS = 8
N_KV_HEADS = 2
Q_PER_KV = N_Q_HEADS // N_KV_HEADS
HEAD_DIM = 128
BLK = 128
ATTN_W = N_Q_HEADS * HEAD_DIM
KV_W = N_KV_HEADS * HEAD_DIM
SG_GROUPS = 4
SG_W = 512
SSM_GROUPS = 32
SSM_CH = 16
SSM_W = 512
SSM_STATE = 64
SSM_SUB = 8
SSM_KT = 4
SSM_NSTATE = SSM_GROUPS * SSM_STATE
MAIN_W = ATTN_W + 2 * KV_W + 2 * SG_W + SSM_W
GATE_W = 3 * D_MODEL
IN_W = MAIN_W + GATE_W
D_FF = 5632
F32_SUBLANES = 8
EPS = 1e-6

VMEM_LIMIT = 56 * 1024 * 1024
MERGE_VMEM_LIMIT = 58 * 1024 * 1024


def _cparams(sem, vmem_limit=VMEM_LIMIT):
    return pltpu.CompilerParams(dimension_semantics=sem, vmem_limit_bytes=vmem_limit)


def _gelu(x):
    c = math.sqrt(2.0 / math.pi)
    return 0.5 * x * (1.0 + jnp.tanh(c * (x + 0.044715 * (x * x * x))))


def _sigmoid(x):
    return 1.0 / (1.0 + jnp.exp(-x))


def _dot(a, b):
    return jnp.dot(a, b, preferred_element_type=F32)


def _row_inv_rms(x):
    return lax.rsqrt(jnp.mean(x * x, axis=-1, keepdims=True) + EPS)


def _rmsnorm_kernel(x_ref, g_ref, o_ref, inv_ref):
    x = x_ref[...]
    o_ref[...] = (x * g_ref[...]).astype(o_ref.dtype)
    inv_ref[...] = _row_inv_rms(x)


def _rmsnorm(x, g, layer, tm=1024):
    n, d = x.shape
    return pl.pallas_call(
        _rmsnorm_kernel,
        grid=(n // tm,),
        in_specs=[pl.BlockSpec((tm, d), lambda i: (i, 0)),
                  pl.BlockSpec((None, 1, d), lambda i: (layer, 0, 0))],
        out_specs=[pl.BlockSpec((tm, d), lambda i: (i, 0)), pl.BlockSpec((tm, 1), lambda i: (i, 0))],
        out_shape=[jax.ShapeDtypeStruct((n, d), BF16), jax.ShapeDtypeStruct((n, 1), F32)],
        compiler_params=_cparams(("parallel",)),
        name="rmsnorm",
    )(x, g)


def _in_proj_kernel(h_ref, inv_ref, w_ref, o_ref, wb_ref):
    @pl.when(pl.program_id(1) == 0)
    def _():
        wb_ref[...] = w_ref[...].astype(BF16)

    o_ref[...] = (_dot(h_ref[...], wb_ref[...]) * inv_ref[...]).astype(o_ref.dtype)


def _in_proj(h, inv, w_in, layer, tm=1024, tn=1536):
    n, d = h.shape
    width = w_in.shape[2]
    nj = width // tn
    main_tiles = MAIN_W // tn
    return pl.pallas_call(
        _in_proj_kernel,
        grid=(nj, n // tm),
        in_specs=[pl.BlockSpec((tm, d), lambda j, i: (i, 0)),
                  pl.BlockSpec((tm, 1), lambda j, i: (i, 0)),
                  pl.BlockSpec((None, d, tn), lambda j, i: (layer, 0, (j + main_tiles) % nj))],
        out_specs=pl.BlockSpec((tm, tn), lambda j, i: (i, j)),
        out_shape=jax.ShapeDtypeStruct((n, width), BF16),
        scratch_shapes=[pltpu.VMEM((d, tn), BF16)],
        compiler_params=_cparams(("parallel", "arbitrary")),
        name="in_proj",
    )(h, inv, w_in)


def _attn_kernel(q_ref, k_ref, v_ref, kp_ref, vp_ref, qg_ref, kg_ref, bias_ref, sink_ref, o_ref,
                 *, blocks_per_step, blocks_per_seq):
    i = pl.program_id(0)
    qg = qg_ref[...] * (HEAD_DIM ** -0.5)
    kg = kg_ref[...]
    row = lax.broadcasted_iota(jnp.int32, (BLK, BLK), 0)
    col = lax.broadcasted_iota(jnp.int32, (BLK, BLK), 1)
    cur4 = jnp.concatenate([col <= row] * Q_PER_KV, axis=0)
    ones = jnp.ones((BLK, BLK), BF16)
    ones_win = jnp.ones((2 * BLK, BLK), BF16)

    def rms(x, g):
        ssq = _dot((x * x).astype(BF16), ones)
        return x * lax.rsqrt(ssq * (1.0 / HEAD_DIM) + EPS) * g

    k_prev = None
    for r in range(blocks_per_step):
        rows = slice(r * BLK, (r + 1) * BLK)
        if r == 0:
            has_prev = (i * blocks_per_step) % blocks_per_seq != 0
            k_prev = [rms(kp_ref[:, h * HEAD_DIM:(h + 1) * HEAD_DIM].astype(F32), kg)
                      for h in range(N_KV_HEADS)]
            v_prev = vp_ref[...]
        else:
            has_prev = None
            v_prev = v_ref[(r - 1) * BLK:r * BLK, :]
        k_cur = []
        for h in range(N_KV_HEADS):
            hc = slice(h * HEAD_DIM, (h + 1) * HEAD_DIM)
            k_cur.append(rms(k_ref[rows, hc].astype(F32), kg))
            kwin = jnp.concatenate([k_prev[h], k_cur[h]], axis=0).astype(BF16)
            vwin = jnp.concatenate([v_prev[:, hc], v_ref[rows, hc]], axis=0)
            vext = jnp.concatenate([vwin, ones_win], axis=1)
            q4 = jnp.concatenate(
                [q_ref[rows, (h * Q_PER_KV + g) * HEAD_DIM:(h * Q_PER_KV + g + 1) * HEAD_DIM].astype(F32)
                 for g in range(Q_PER_KV)], axis=0)
            q4 = rms(q4, qg).astype(BF16)
            s2 = lax.dot_general(q4, kwin, (((1,), (1,)), ((), ())),
                                 preferred_element_type=F32)
            s = jnp.where(cur4, s2[:, BLK:], s2[:, :BLK]) - bias_ref[h]
            if has_prev is not None:
                s = jnp.where(jnp.logical_or(cur4, has_prev), s, -jnp.inf)
            sink = sink_ref[h]
            m = jnp.maximum(jnp.max(s, axis=-1, keepdims=True), sink)
            p = jnp.exp(s - m)
            p2 = jnp.concatenate([jnp.where(cur4, 0.0, p), jnp.where(cur4, p, 0.0)],
                                 axis=1).astype(BF16)
            ov = _dot(p2, vext)
            o = ov[:, :BLK] / (ov[:, BLK:] + jnp.exp(sink - m))
            for g in range(Q_PER_KV):
                c0 = (h * Q_PER_KV + g) * HEAD_DIM
                o_ref[rows, c0:c0 + HEAD_DIM] = o[g * BLK:(g + 1) * BLK].astype(o_ref.dtype)
        k_prev = k_cur


def _sg_kernel(u_ref, v_ref, lng_ref, lnb_ref, w_ref, b_ref, o_ref, *, chunks_per_step):
    row = lax.broadcasted_iota(jnp.int32, (BLK, BLK), 0)
    col = lax.broadcasted_iota(jnp.int32, (BLK, BLK), 1)
    tril = col <= row
    for c in range(chunks_per_step):
        rows = slice(c * BLK, (c + 1) * BLK)
        for g in range(SG_GROUPS):
            gc = slice(g * BLK, (g + 1) * BLK)
            zv = _gelu(v_ref[rows, gc].astype(F32))
            mu = jnp.mean(zv, axis=-1, keepdims=True)
            zc = zv - mu
            var = jnp.mean(zc * zc, axis=-1, keepdims=True)
            zn = zc * lax.rsqrt(var + EPS) * lng_ref[:, gc] + lnb_ref[:, gc]
            w = jnp.where(tril, w_ref[g], 0.0).astype(BF16)
            mixed = _dot(w, zn.astype(BF16)) + b_ref[g]
            zu = _gelu(u_ref[rows, gc].astype(F32))
            o_ref[rows, gc] = (zu * mixed).astype(o_ref.dtype)


def _ssm_operators(a_re, a_im, log_dt, b_re, b_im, c_re, c_im):
    hp = lax.Precision.HIGHEST
    p, c, kt, sub = SSM_STATE, SSM_CH, SSM_KT, SSM_SUB
    gl = SSM_GROUPS // kt
    nl = a_re.shape[0]
    a_re, a_im = a_re.astype(F32)[:, None], a_im.astype(F32)[:, None]
    dt = jnp.exp(log_dt.astype(F32))[:, None, :, None]
    k = jnp.arange(sub + 1, dtype=F32)[None, :, None, None]
    mag = jnp.exp(a_re * dt * k)
    pr, pi = mag * jnp.cos(a_im * dt * k), mag * jnp.sin(a_im * dt * k)
    ar, ai, lr, li = pr[:, 1], pi[:, 1], a_re[:, 0], a_im[:, 0]
    den = lr * lr + li * li
    cr = ((ar - 1.0) * lr + ai * li) / den
    ci = (ai * lr - (ar - 1.0) * li) / den
    bbr = cr[..., None] * b_re - ci[..., None] * b_im
    bbi = cr[..., None] * b_im + ci[..., None] * b_re
    bbr_t, bbi_t = jnp.swapaxes(bbr, -1, -2), jnp.swapaxes(bbi, -1, -2)
    car = c_re[:, None] * pr[:, :, :, None, :] - c_im[:, None] * pi[:, :, :, None, :]
    cai = c_re[:, None] * pi[:, :, :, None, :] + c_im[:, None] * pr[:, :, :, None, :]

    def expand(x, rows_per_group, blocks, width):
        col = jnp.arange(blocks * gl * width)
        src = (col // (gl * width)) * width + col % width
        spread = (jnp.arange(blocks * width)[:, None] == src[None, :]).astype(BF16)
        row_group = (jnp.arange(x.shape[2]) // rows_per_group) % gl
        mask = (row_group[:, None] == ((col // width) % gl)[None, :]).astype(BF16)
        return jnp.einsum("xkrm,mn->xkrn", x.astype(BF16), spread, preferred_element_type=BF16) * mask

    kk = (jnp.einsum("xgdp,xlgcp->xlgdc", bbr_t, car[:, :sub], precision=hp)
          - jnp.einsum("xgdp,xlgcp->xlgdc", bbi_t, cai[:, :sub], precision=hp))
    krow = kk.reshape(nl, sub, kt, gl * c, c).transpose(0, 2, 3, 1, 4).reshape(nl, kt, gl * c, sub * c)
    krow = jnp.pad(krow, ((0, 0), (0, 0), (0, 0), (sub * c, 0)))
    kst = jnp.concatenate([krow[..., (sub - s) * c:(2 * sub - s) * c] for s in range(sub)], axis=2)
    toep = expand(kst, c, sub, c)

    rr, ri = pr[:, sub - 1::-1][:, :, :, None, :], pi[:, sub - 1::-1][:, :, :, None, :]
    wr = rr * bbr_t[:, None] - ri * bbi_t[:, None]
    wi = rr * bbi_t[:, None] + ri * bbr_t[:, None]

    def inject(x):
        x = x.reshape(nl, sub, kt, gl * c, p).transpose(0, 2, 1, 3, 4)
        return expand(x.reshape(nl, kt, sub * gl * c, p), c, 1, p)

    w2 = jnp.concatenate([inject(wr), inject(wi)], axis=-1)

    def readout(x):
        x = x.reshape(nl, sub, kt, gl, c, p).transpose(0, 2, 3, 5, 1, 4)
        return expand(x.reshape(nl, kt, gl * p, sub * c), p, sub, c)

    cre = readout(car[:, 1:])
    cim = readout(-cai[:, 1:])
    a8 = jnp.stack([pr[:, sub].reshape(nl, -1), pi[:, sub].reshape(nl, -1)], axis=1)
    return toep, w2, cre, cim, a8


def _ssm_kernel(x_ref, toep_ref, w2_ref, cre_ref, cim_ref, a8_ref, d_ref, wg_ref, bg_ref, o_ref,
                slab_ref, st_ref, carry_ref, *, tiles_per_seq):
    i = pl.program_id(0)
    tm = x_ref.shape[0]
    nj = tm // SSM_SUB
    ns = SSM_NSTATE
    gw = ns // SSM_KT

    @pl.when(i % tiles_per_seq == 0)
    def _():
        carry_ref[...] = jnp.zeros_like(carry_ref)

    xall = []
    for kt in range(SSM_KT):
        slab_ref[kt] = x_ref[:, kt * BLK:(kt + 1) * BLK].astype(F32)
        xall.append(jnp.concatenate(
            [slab_ref[kt, pl.ds(s, nj, stride=SSM_SUB), :].astype(BF16) for s in range(SSM_SUB)], axis=1))

    for kt in range(SSM_KT):
        acc = _dot(xall[kt], w2_ref[kt])
        st_ref[:, kt * gw:(kt + 1) * gw] = acc[:, :gw]
        st_ref[:, ns + kt * gw:ns + (kt + 1) * gw] = acc[:, gw:]

    a8r = a8_ref[0:1, :]
    a8i = a8_ref[1:2, :]

    def step(j, st):
        sr, si = st
        lr = st_ref[pl.ds(j, 1), :ns]
        li = st_ref[pl.ds(j, 1), ns:]
        st_ref[pl.ds(j, 1), :ns] = sr
        st_ref[pl.ds(j, 1), ns:] = si
        return (a8r * sr - a8i * si + lr, a8r * si + a8i * sr + li)

    sr, si = lax.fori_loop(0, nj, step, (carry_ref[0:1, :], carry_ref[1:2, :]), unroll=8)
    carry_ref[0:1, :] = sr
    carry_ref[1:2, :] = si

    stb = st_ref[...].astype(BF16)
    for kt in range(SSM_KT):
        y = (_dot(xall[kt], toep_ref[kt]) + _dot(stb[:, kt * gw:(kt + 1) * gw], cre_ref[kt])
             + _dot(stb[:, ns + kt * gw:ns + (kt + 1) * gw], cim_ref[kt]))
        for t in range(SSM_SUB):
            slab_ref[kt, pl.ds(t, nj, stride=SSM_SUB), :] = y[:, t * BLK:(t + 1) * BLK]
    y = jnp.concatenate([slab_ref[kt] for kt in range(SSM_KT)], axis=1)
    y = _gelu(y + d_ref[...] * x_ref[...].astype(F32))
    o_ref[...] = (y * _sigmoid(_dot(y.astype(BF16), wg_ref[...]) + bg_ref[...])).astype(o_ref.dtype)


MIX_BLOCKS = 8


def _mixers_kernel(q_ref, k_ref, v_ref, kp_ref, vp_ref, qg_ref, kg_ref, bias_ref, sink_ref,
                   u_ref, sv_ref, lng_ref, lnb_ref, ws_ref, bs_ref,
                   x_ref, toep_ref, w2_ref, cre_ref, cim_ref, a8_ref, d_ref, wg_ref, bg_ref,
                   ya_ref, ysg_ref, yssm_ref, slab_ref, st_ref, carry_ref, *, blocks_per_seq):
    _ssm_kernel(x_ref, toep_ref, w2_ref, cre_ref, cim_ref, a8_ref, d_ref, wg_ref, bg_ref, yssm_ref,
                slab_ref, st_ref, carry_ref, tiles_per_seq=blocks_per_seq // MIX_BLOCKS)
    _attn_kernel(q_ref, k_ref, v_ref, kp_ref, vp_ref, qg_ref, kg_ref, bias_ref, sink_ref, ya_ref,
                 blocks_per_step=MIX_BLOCKS, blocks_per_seq=blocks_per_seq)
    _sg_kernel(u_ref, sv_ref, lng_ref, lnb_ref, ws_ref, bs_ref, ysg_ref, chunks_per_step=MIX_BLOCKS)


def _mixers(proj, layer, seq_len, q_gain, k_gain, sinks, sg_ln_g, sg_ln_b, sg_w, sg_b,
            ssm_ops, d_skip, w_glu, b_glu):
    n = proj.shape[0]
    tm = MIX_BLOCKS * BLK
    nj = tm // SSM_SUB
    toep, w2, cre, cim, a8 = ssm_ops
    slopes = 2.0 ** (-8.0 * jnp.arange(1, N_Q_HEADS + 1, dtype=F32) / N_Q_HEADS)
    ri = jnp.arange(BLK)[:, None]
    ci = jnp.arange(BLK)[None, :]
    dist = jnp.where(ci <= ri, ri - ci, ri - ci + BLK).astype(F32)
    bias = (slopes[:, None, None] * dist).reshape(N_KV_HEADS, Q_PER_KV * BLK, BLK)
    sink_tile = jnp.broadcast_to(sinks.astype(F32)[:, None, None], (N_Q_HEADS, BLK, BLK))
    sink_tile = sink_tile.reshape(N_KV_HEADS, Q_PER_KV * BLK, BLK)

    def cols(width, off):
        return pl.BlockSpec((tm, width), lambda i: (i, off // width))

    def prev_block(off):
        return pl.BlockSpec((BLK, KV_W), lambda i: (jnp.maximum(i * MIX_BLOCKS - 1, 0), off // KV_W))

    def const(arr):
        return pl.BlockSpec(arr.shape, lambda i: (0,) * arr.ndim)

    def layer_slab(arr):
        nd = arr.ndim - 1
        return pl.BlockSpec((None,) + arr.shape[1:], lambda i: (layer,) + (0,) * nd,
                            pipeline_mode=pl.Buffered(1))

    q_off = GATE_W
    k_off, v_off = q_off + ATTN_W, q_off + ATTN_W + KV_W
    u_off, sv_off, x_off = v_off + KV_W, v_off + KV_W + SG_W, v_off + KV_W + 2 * SG_W
    attn_consts = (q_gain.reshape(1, HEAD_DIM).astype(F32), k_gain.reshape(1, HEAD_DIM).astype(F32),
                   bias, sink_tile)
    sg_consts = (sg_ln_g.reshape(1, SG_W).astype(F32), sg_ln_b.reshape(1, SG_W).astype(F32),
                 sg_w.astype(F32), sg_b.astype(F32)[..., None])
    ssm_consts = (toep, w2, cre, cim, a8, d_skip, w_glu, b_glu)
    row = lambda w: pl.BlockSpec((tm, w), lambda i: (i, 0))
    return pl.pallas_call(
        functools.partial(_mixers_kernel, blocks_per_seq=seq_len // BLK),
        grid=(n // tm,),
        in_specs=[cols(ATTN_W, q_off), cols(KV_W, k_off), cols(KV_W, v_off), prev_block(k_off), prev_block(v_off)]
                 + [const(a) for a in attn_consts]
                 + [cols(SG_W, u_off), cols(SG_W, sv_off)] + [const(a) for a in sg_consts]
                 + [cols(SSM_W, x_off)] + [layer_slab(a) for a in ssm_consts],
        out_specs=[row(ATTN_W), row(SG_W), row(SSM_W)],
        out_shape=[jax.ShapeDtypeStruct((n, w), BF16) for w in (ATTN_W, SG_W, SSM_W)],
        scratch_shapes=[pltpu.VMEM((SSM_KT, tm, BLK), F32),
                        pltpu.VMEM((nj, 2 * SSM_NSTATE), F32),
                        pltpu.VMEM((2, SSM_NSTATE), F32)],
        compiler_params=_cparams(("arbitrary",)),
        name="mixers",
    )(proj, proj, proj, proj, proj, *attn_consts, proj, proj, *sg_consts, proj, *ssm_consts)


def _merge_kernel(x_ref, g0_ref, g1_ref, g2_ref, ya_ref, ys_ref, ym_ref,
                  bg_ref, wa_ref, ws_ref, wm_ref, wo_ref, n2_ref, xo_ref, h_ref):
    d = D_MODEL

    def gate(g_ref, b):
        return _sigmoid(g_ref[...].astype(F32) + bg_ref[:, b * d:(b + 1) * d])

    merged = gate(g0_ref, 0) * _dot(ya_ref[...], wa_ref[...])
    merged = merged + gate(g1_ref, 1) * _dot(ys_ref[...], ws_ref[...])
    merged = merged + gate(g2_ref, 2) * _dot(ym_ref[...], wm_ref[...])
    xn = x_ref[...] + _dot(merged.astype(BF16), wo_ref[...])
    xo_ref[...] = xn
    h_ref[...] = (xn * _row_inv_rms(xn) * n2_ref[...]).astype(h_ref.dtype)


def _merge(x, proj, y_attn, y_sg, y_ssm, b_gate, wa, ws, wm, wo, norm2_g, layer, tm=512):
    n, d = x.shape
    row = lambda w: pl.BlockSpec((tm, w), lambda i: (i, 0))
    gate = lambda b: pl.BlockSpec((tm, d), lambda i: (i, b))

    def res(arr):
        nd = arr.ndim - 1
        return pl.BlockSpec((None,) + arr.shape[1:], lambda i: (layer,) + (0,) * nd,
                            pipeline_mode=pl.Buffered(1))

    return pl.pallas_call(
        _merge_kernel,
        grid=(n // tm,),
        in_specs=[row(d)] + [gate(b) for b in range(3)]
                 + [row(ATTN_W), row(SG_W), row(SSM_W),
                    res(b_gate), res(wa), res(ws), res(wm), res(wo), res(norm2_g)],
        out_specs=[row(d), row(d)],
        out_shape=[jax.ShapeDtypeStruct((n, d), F32), jax.ShapeDtypeStruct((n, d), BF16)],
        compiler_params=_cparams(("parallel",), MERGE_VMEM_LIMIT),
        name="merge_outproj",
    )(x, proj, proj, proj, y_attn, y_sg, y_ssm, b_gate, wa, ws, wm, wo, norm2_g)


def _ffn_up_kernel(h_ref, wg_ref, wv_ref, cwg_ref, cwv_ref, cbg_ref, cbv_ref, o_ref,
                   wb_ref, e_ref, *, tiles_per_seq, row_chunks):
    i = pl.program_id(1)
    tm = h_ref.shape[0]
    tn = o_ref.shape[1]
    tc = tm // row_chunks
    halo = F32_SUBLANES
    nslab = tn // BLK

    @pl.when(i == 0)
    def _():
        wb_ref[:, :tn] = wg_ref[...].astype(BF16)
        wb_ref[:, tn:] = wv_ref[...].astype(BF16)

    @pl.when(i % tiles_per_seq == 0)
    def _():
        e_ref[:, 0:halo, :] = jnp.zeros((2 * nslab, halo, BLK), F32)

    def conv(slab, base, cw_ref, cb_ref, lanes):
        return (cb_ref[:, lanes] + cw_ref[0:1, lanes] * e_ref[slab, base:base + tc, :]
                + cw_ref[1:2, lanes] * e_ref[slab, base - 1:base - 1 + tc, :]
                + cw_ref[2:3, lanes] * e_ref[slab, base - 2:base - 2 + tc, :])

    for r in range(row_chunks):
        rows = slice(r * tc, (r + 1) * tc)
        base = halo + r * tc
        e = _dot(h_ref[rows, :], wb_ref[...])
        for s in range(2 * nslab):
            e_ref[s, base:base + tc, :] = e[:, s * BLK:(s + 1) * BLK]
        for s in range(nslab):
            lanes = slice(s * BLK, (s + 1) * BLK)
            gate = conv(s, base, cwg_ref, cbg_ref, lanes)
            val = conv(nslab + s, base, cwv_ref, cbv_ref, lanes)
            o_ref[rows, lanes] = (_gelu(gate) * val).astype(o_ref.dtype)
    e_ref[:, 0:halo, :] = e_ref[:, tm:tm + halo, :]


def _ffn_up(h, w_up, conv_w, conv_b, layer, seq_len, tm=2048, tn=512, row_chunks=2):
    n, d = h.shape
    nj = D_FF // tn
    wspec = lambda off: pl.BlockSpec((None, d, tn), lambda j, i: (layer, 0, j + off))
    cspec = lambda rows, off: pl.BlockSpec((None, rows, tn), lambda j, i: (layer, 0, j + off))
    return pl.pallas_call(
        functools.partial(_ffn_up_kernel, tiles_per_seq=seq_len // tm, row_chunks=row_chunks),
        grid=(nj, n // tm),
        in_specs=[pl.BlockSpec((tm, d), lambda j, i: (i, 0)),
                  wspec(0), wspec(nj), cspec(3, 0), cspec(3, nj), cspec(1, 0), cspec(1, nj)],
        out_specs=pl.BlockSpec((tm, tn), lambda j, i: (i, j)),
        out_shape=jax.ShapeDtypeStruct((n, D_FF), BF16),
        scratch_shapes=[pltpu.VMEM((d, 2 * tn), BF16),
                        pltpu.VMEM((2 * tn // BLK, F32_SUBLANES + tm, BLK), F32)],
        compiler_params=_cparams(("parallel", "arbitrary")),
        name="ffn_up_conv",
    )(h, w_up, w_up, conv_w, conv_w, conv_b, conv_b)


def _ffn_down_kernel(a_ref, w_ref, x_ref, xo_ref):
    xo_ref[...] = x_ref[...] + _dot(a_ref[...], w_ref[...])


def _ffn_down_norm_kernel(a_ref, w_ref, x_ref, g_ref, xo_ref, h_ref, inv_ref, ssq_ref):
    j = pl.program_id(1)
    xn = x_ref[...] + _dot(a_ref[...], w_ref[...])
    xo_ref[...] = xn
    h_ref[...] = (xn * g_ref[...]).astype(h_ref.dtype)

    @pl.when(j == 0)
    def _():
        ssq_ref[...] = jnp.zeros_like(ssq_ref)

    sq = xn * xn
    part = sq[:, 0:BLK]
    for c in range(1, sq.shape[1] // BLK):
        part = part + sq[:, c * BLK:(c + 1) * BLK]
    ssq_ref[...] += part

    @pl.when(j == pl.num_programs(1) - 1)
    def _():
        ssq = jnp.sum(ssq_ref[...], axis=-1, keepdims=True)
        inv_ref[...] = lax.rsqrt(ssq * (1.0 / D_MODEL) + EPS)


FFN_DOWN_TN = 512


def _cast_kernel(w_ref, o_ref):
    o_ref[...] = w_ref[...].astype(o_ref.dtype)


def _ffn_down_weights(w_down, row_blocks=4):
    nl, k, d = w_down.shape
    tn = FFN_DOWN_TN
    tk = k // row_blocks
    return pl.pallas_call(
        _cast_kernel,
        grid=(nl, d // tn, row_blocks),
        in_specs=[pl.BlockSpec((None, tk, tn), lambda l, j, r: (l, r, j))],
        out_specs=pl.BlockSpec((None, None, tk, tn), lambda l, j, r: (l, j, r, 0)),
        out_shape=jax.ShapeDtypeStruct((nl, d // tn, k, tn), BF16),
        compiler_params=_cparams(("parallel", "parallel", "parallel")),
        name="ffn_down_weight_tiles",
    )(w_down)


def _ffn_down(act, w_down, x, norm_g, layer, next_layer, tm=1024):
    n, d = x.shape
    k = act.shape[1]
    tn = FFN_DOWN_TN
    tile = pl.BlockSpec((tm, tn), lambda i, j: (i, j))
    in_specs = [pl.BlockSpec((tm, k), lambda i, j: (i, 0)),
                pl.BlockSpec((None, None, k, tn), lambda i, j: (layer, j, 0, 0)),
                tile]
    common = dict(grid=(n // tm, d // tn), compiler_params=_cparams(("parallel", "arbitrary")))
    x_shape = jax.ShapeDtypeStruct((n, d), F32)
    if next_layer is None:
        xo = pl.pallas_call(_ffn_down_kernel, in_specs=in_specs, out_specs=tile, out_shape=x_shape,
                            name="ffn_down_last", **common)(act, w_down, x)
        return xo, None, None
    return pl.pallas_call(
        _ffn_down_norm_kernel,
        in_specs=in_specs + [pl.BlockSpec((None, 1, tn), lambda i, j: (next_layer, 0, j))],
        out_specs=[tile, tile, pl.BlockSpec((tm, 1), lambda i, j: (i, 0))],
        out_shape=[x_shape, jax.ShapeDtypeStruct((n, d), BF16), jax.ShapeDtypeStruct((n, 1), F32)],
        scratch_shapes=[pltpu.VMEM((tm, BLK), F32)],
        name="ffn_down",
        **common,
    )(act, w_down, x, norm_g)


def kernel(x, norm1_g, w_in, b_gate, q_norm_g, k_norm_g, attn_sinks, sg_ln_g, sg_ln_b, sg_w, sg_b, ssm_a_re, ssm_a_im, ssm_log_dt, ssm_b_re, ssm_b_im, ssm_c_re, ssm_c_im, ssm_d, ssm_w_glu, ssm_b_glu, w_proj_attn, w_proj_sg, w_proj_ssm, w_out, norm2_g, ffn_w_up, ffn_conv_w, ffn_conv_b, ffn_w_down):
    b, seq_len, d = x.shape
    depth = w_in.shape[0]
    xr = x.reshape(b * seq_len, d)
    ssm_ops = _ssm_operators(ssm_a_re, ssm_a_im, ssm_log_dt, ssm_b_re, ssm_b_im, ssm_c_re, ssm_c_im)
    w_glu, wa, ws, wm, wo = (w.astype(BF16) for w in (
        ssm_w_glu, w_proj_attn, w_proj_sg, w_proj_ssm, w_out))
    w_down = _ffn_down_weights(ffn_w_down)
    rowvec = lambda v: v.astype(F32).reshape(depth, 1, -1)
    ssm_d_r, b_glu_r, b_gate_r, norm1_r, norm2_r, conv_b_r = (rowvec(v) for v in (
        ssm_d, ssm_b_glu, b_gate, norm1_g, norm2_g, ffn_conv_b))

    h, inv = _rmsnorm(xr, norm1_r, 0)
    for l in range(depth):
        proj = _in_proj(h, inv, w_in, l)
        y_attn, y_sg, y_ssm = _mixers(proj, l, seq_len, q_norm_g[l], k_norm_g[l], attn_sinks[l],
                                      sg_ln_g[l], sg_ln_b[l], sg_w[l], sg_b[l],
                                      ssm_ops, ssm_d_r, w_glu, b_glu_r)
        xr, h2 = _merge(xr, proj, y_attn, y_sg, y_ssm, b_gate_r, wa, ws, wm, wo, norm2_r, l)
        act = _ffn_up(h2, ffn_w_up, ffn_conv_w, conv_b_r, l, seq_len)
        xr, h, inv = _ffn_down(act, w_down, xr, norm1_r, l, l + 1 if l + 1 < depth else None)
    return xr.reshape(b, seq_len, d)
```

```python
import functools
import math

import jax
import jax.numpy as jnp
from jax import lax
from jax.experimental import pallas as pl
from jax.experimental.pallas import tpu as pltpu

F32 = jnp.float32
BF16 = jnp.bfloat16

D_MODEL = 2048
N_Q_HEADS = 8
N_KV_HEADS = 2
Q_PER_KV = N_Q_HEADS // N_KV_HEADS
HEAD_DIM = 128
BLK = 128
ATTN_W = N_Q_HEADS * HEAD_DIM
KV_W = N_KV_HEADS * HEAD_DIM
SG_GROUPS = 4
SG_W = 512
SSM_GROUPS = 32
SSM_CH = 16
SSM_W = 512
SSM_STATE = 64
SSM_SUB = 8
SSM_KT = 4
SSM_NSTATE = SSM_GROUPS * SSM_STATE
MAIN_W = ATTN_W + 2 * KV_W + 2 * SG_W + SSM_W
GATE_W = 3 * D_MODEL
IN_W = MAIN_W + GATE_W
D_FF = 5632
F32_SUBLANES = 8
EPS = 1e-6

VMEM_LIMIT = 56 * 1024 * 1024
MERGE_VMEM_LIMIT = 58 * 1024 * 1024


def _cparams(sem, vmem_limit=VMEM_LIMIT):
    return pltpu.CompilerParams(dimension_semantics=sem, vmem_limit_bytes=vmem_limit)


def _gelu(x):
    c = math.sqrt(2.0 / math.pi)
    return 0.5 * x * (1.0 + jnp.tanh(c * (x + 0.044715 * (x * x * x))))


def _sigmoid(x):
    return 1.0 / (1.0 + jnp.exp(-x))


def _dot(a, b):
    return jnp.dot(a, b, preferred_element_type=F32)


def _row_inv_rms(x):
    return lax.rsqrt(jnp.mean(x * x, axis=-1, keepdims=True) + EPS)


def _rmsnorm_kernel(x_ref, g_ref, o_ref, inv_ref):
    x = x_ref[...]
    o_ref[...] = (x * g_ref[...]).astype(o_ref.dtype)
    inv_ref[...] = _row_inv_rms(x)


def _rmsnorm(x, g, layer, tm=1024):
    n, d = x.shape
    return pl.pallas_call(
        _rmsnorm_kernel,
        grid=(n // tm,),
        in_specs=[pl.BlockSpec((tm, d), lambda i: (i, 0)),
                  pl.BlockSpec((None, 1, d), lambda i: (layer, 0, 0))],
        out_specs=[pl.BlockSpec((tm, d), lambda i: (i, 0)), pl.BlockSpec((tm, 1), lambda i: (i, 0))],
        out_shape=[jax.ShapeDtypeStruct((n, d), BF16), jax.ShapeDtypeStruct((n, 1), F32)],
        compiler_params=_cparams(("parallel",)),
        name="rmsnorm",
    )(x, g)


def _in_proj_kernel(h_ref, inv_ref, w_ref, o_ref, wb_ref):
    @pl.when(pl.program_id(1) == 0)
    def _():
        wb_ref[...] = w_ref[...].astype(BF16)

    o_ref[...] = (_dot(h_ref[...], wb_ref[...]) * inv_ref[...]).astype(o_ref.dtype)


def _in_proj(h, inv, w_in, layer, tm=1024, tn=1536):
    n, d = h.shape
    width = w_in.shape[2]
    nj = width // tn
    main_tiles = MAIN_W // tn
    return pl.pallas_call(
        _in_proj_kernel,
        grid=(nj, n // tm),
        in_specs=[pl.BlockSpec((tm, d), lambda j, i: (i, 0)),
                  pl.BlockSpec((tm, 1), lambda j, i: (i, 0)),
                  pl.BlockSpec((None, d, tn), lambda j, i: (layer, 0, (j + main_tiles) % nj))],
        out_specs=pl.BlockSpec((tm, tn), lambda j, i: (i, j)),
        out_shape=jax.ShapeDtypeStruct((n, width), BF16),
        scratch_shapes=[pltpu.VMEM((d, tn), BF16)],
        compiler_params=_cparams(("parallel", "arbitrary")),
        name="in_proj",
    )(h, inv, w_in)


def _attn_kernel(q_ref, k_ref, v_ref, kp_ref, vp_ref, qg_ref, kg_ref, bias_ref, sink_ref, o_ref,
                 *, blocks_per_step, blocks_per_seq):
    i = pl.program_id(0)
    qg = qg_ref[...] * (HEAD_DIM ** -0.5)
    kg = kg_ref[...]
    row = lax.broadcasted_iota(jnp.int32, (BLK, BLK), 0)
    col = lax.broadcasted_iota(jnp.int32, (BLK, BLK), 1)
    cur4 = jnp.concatenate([col <= row] * Q_PER_KV, axis=0)
    ones = jnp.ones((BLK, BLK), BF16)
    ones_win = jnp.ones((2 * BLK, BLK), BF16)

    def rms(x, g):
        ssq = _dot((x * x).astype(BF16), ones)
        return x * lax.rsqrt(ssq * (1.0 / HEAD_DIM) + EPS) * g

    k_prev = None
    for r in range(blocks_per_step):
        rows = slice(r * BLK, (r + 1) * BLK)
        if r == 0:
            has_prev = (i * blocks_per_step) % blocks_per_seq != 0
            k_prev = [rms(kp_ref[:, h * HEAD_DIM:(h + 1) * HEAD_DIM].astype(F32), kg)
                      for h in range(N_KV_HEADS)]
            v_prev = vp_ref[...]
        else:
            has_prev = None
            v_prev = v_ref[(r - 1) * BLK:r * BLK, :]
        k_cur = []
        for h in range(N_KV_HEADS):
            hc = slice(h * HEAD_DIM, (h + 1) * HEAD_DIM)
            k_cur.append(rms(k_ref[rows, hc].astype(F32), kg))
            kwin = jnp.concatenate([k_prev[h], k_cur[h]], axis=0).astype(BF16)
            vwin = jnp.concatenate([v_prev[:, hc], v_ref[rows, hc]], axis=0)
            vext = jnp.concatenate([vwin, ones_win], axis=1)
            q4 = jnp.concatenate(
                [q_ref[rows, (h * Q_PER_KV + g) * HEAD_DIM:(h * Q_PER_KV + g + 1) * HEAD_DIM].astype(F32)
                 for g in range(Q_PER_KV)], axis=0)
            q4 = rms(q4, qg).astype(BF16)
            s2 = lax.dot_general(q4, kwin, (((1,), (1,)), ((), ())),
                                 preferred_element_type=F32)
            s = jnp.where(cur4, s2[:, BLK:], s2[:, :BLK]) - bias_ref[h]
            if has_prev is not None:
                s = jnp.where(jnp.logical_or(cur4, has_prev), s, -jnp.inf)
            sink = sink_ref[h]
            m = jnp.maximum(jnp.max(s, axis=-1, keepdims=True), sink)
            p = jnp.exp(s - m)
            p2 = jnp.concatenate([jnp.where(cur4, 0.0, p), jnp.where(cur4, p, 0.0)],
                                 axis=1).astype(BF16)
            ov = _dot(p2, vext)
            o = ov[:, :BLK] / (ov[:, BLK:] + jnp.exp(sink - m))
            for g in range(Q_PER_KV):
                c0 = (h * Q_PER_KV + g) * HEAD_DIM
                o_ref[rows, c0:c0 + HEAD_DIM] = o[g * BLK:(g + 1) * BLK].astype(o_ref.dtype)
        k_prev = k_cur


def _sg_kernel(u_ref, v_ref, lng_ref, lnb_ref, w_ref, b_ref, o_ref, *, chunks_per_step):
    row = lax.broadcasted_iota(jnp.int32, (BLK, BLK), 0)
    col = lax.broadcasted_iota(jnp.int32, (BLK, BLK), 1)
    tril = col <= row
    for c in range(chunks_per_step):
        rows = slice(c * BLK, (c + 1) * BLK)
        for g in range(SG_GROUPS):
            gc = slice(g * BLK, (g + 1) * BLK)
            zv = _gelu(v_ref[rows, gc].astype(F32))
            mu = jnp.mean(zv, axis=-1, keepdims=True)
            zc = zv - mu
            var = jnp.mean(zc * zc, axis=-1, keepdims=True)
            zn = zc * lax.rsqrt(var + EPS) * lng_ref[:, gc] + lnb_ref[:, gc]
            w = jnp.where(tril, w_ref[g], 0.0).astype(BF16)
            mixed = _dot(w, zn.astype(BF16)) + b_ref[g]
            zu = _gelu(u_ref[rows, gc].astype(F32))
            o_ref[rows, gc] = (zu * mixed).astype(o_ref.dtype)


def _ssm_operators(a_re, a_im, log_dt, b_re, b_im, c_re, c_im):
    hp = lax.Precision.HIGHEST
    p, c, kt, sub = SSM_STATE, SSM_CH, SSM_KT, SSM_SUB
    gl = SSM_GROUPS // kt
    nl = a_re.shape[0]
    a_re, a_im = a_re.astype(F32)[:, None], a_im.astype(F32)[:, None]
    dt = jnp.exp(log_dt.astype(F32))[:, None, :, None]
    k = jnp.arange(sub + 1, dtype=F32)[None, :, None, None]
    mag = jnp.exp(a_re * dt * k)
    pr, pi = mag * jnp.cos(a_im * dt * k), mag * jnp.sin(a_im * dt * k)
    ar, ai, lr, li = pr[:, 1], pi[:, 1], a_re[:, 0], a_im[:, 0]
    den = lr * lr + li * li
    cr = ((ar - 1.0) * lr + ai * li) / den
    ci = (ai * lr - (ar - 1.0) * li) / den
    bbr = cr[..., None] * b_re - ci[..., None] * b_im
    bbi = cr[..., None] * b_im + ci[..., None] * b_re
    bbr_t, bbi_t = jnp.swapaxes(bbr, -1, -2), jnp.swapaxes(bbi, -1, -2)
    car = c_re[:, None] * pr[:, :, :, None, :] - c_im[:, None] * pi[:, :, :, None, :]
    cai = c_re[:, None] * pi[:, :, :, None, :] + c_im[:, None] * pr[:, :, :, None, :]

    def expand(x, rows_per_group, blocks, width):
        col = jnp.arange(blocks * gl * width)
        src = (col // (gl * width)) * width + col % width
        spread = (jnp.arange(blocks * width)[:, None] == src[None, :]).astype(BF16)
        row_group = (jnp.arange(x.shape[2]) // rows_per_group) % gl
        mask = (row_group[:, None] == ((col // width) % gl)[None, :]).astype(BF16)
        return jnp.einsum("xkrm,mn->xkrn", x.astype(BF16), spread, preferred_element_type=BF16) * mask

    kk = (jnp.einsum("xgdp,xlgcp->xlgdc", bbr_t, car[:, :sub], precision=hp)
          - jnp.einsum("xgdp,xlgcp->xlgdc", bbi_t, cai[:, :sub], precision=hp))
    krow = kk.reshape(nl, sub, kt, gl * c, c).transpose(0, 2, 3, 1, 4).reshape(nl, kt, gl * c, sub * c)
    krow = jnp.pad(krow, ((0, 0), (0, 0), (0, 0), (sub * c, 0)))
    kst = jnp.concatenate([krow[..., (sub - s) * c:(2 * sub - s) * c] for s in range(sub)], axis=2)
    toep = expand(kst, c, sub, c)

    rr, ri = pr[:, sub - 1::-1][:, :, :, None, :], pi[:, sub - 1::-1][:, :, :, None, :]
    wr = rr * bbr_t[:, None] - ri * bbi_t[:, None]
    wi = rr * bbi_t[:, None] + ri * bbr_t[:, None]

    def inject(x):
        x = x.reshape(nl, sub, kt, gl * c, p).transpose(0, 2, 1, 3, 4)
        return expand(x.reshape(nl, kt, sub * gl * c, p), c, 1, p)

    w2 = jnp.concatenate([inject(wr), inject(wi)], axis=-1)

    def readout(x):
        x = x.reshape(nl, sub, kt, gl, c, p).transpose(0, 2, 3, 5, 1, 4)
        return expand(x.reshape(nl, kt, gl * p, sub * c), p, sub, c)

    cre = readout(car[:, 1:])
    cim = readout(-cai[:, 1:])
    a8 = jnp.stack([pr[:, sub].reshape(nl, -1), pi[:, sub].reshape(nl, -1)], axis=1)
    return toep, w2, cre, cim, a8


def _ssm_kernel(x_ref, toep_ref, w2_ref, cre_ref, cim_ref, a8_ref, d_ref, wg_ref, bg_ref, o_ref,
                slab_ref, st_ref, carry_ref, *, tiles_per_seq):
    i = pl.program_id(0)
    tm = x_ref.shape[0]
    nj = tm // SSM_SUB
    ns = SSM_NSTATE
    gw = ns // SSM_KT

    @pl.when(i % tiles_per_seq == 0)
    def _():
        carry_ref[...] = jnp.zeros_like(carry_ref)

    xall = []
    for kt in range(SSM_KT):
        slab_ref[kt] = x_ref[:, kt * BLK:(kt + 1) * BLK].astype(F32)
        xall.append(jnp.concatenate(
            [slab_ref[kt, pl.ds(s, nj, stride=SSM_SUB), :].astype(BF16) for s in range(SSM_SUB)], axis=1))

    for kt in range(SSM_KT):
        acc = _dot(xall[kt], w2_ref[kt])
        st_ref[:, kt * gw:(kt + 1) * gw] = acc[:, :gw]
        st_ref[:, ns + kt * gw:ns + (kt + 1) * gw] = acc[:, gw:]

    a8r = a8_ref[0:1, :]
    a8i = a8_ref[1:2, :]

    def step(j, st):
        sr, si = st
        lr = st_ref[pl.ds(j, 1), :ns]
        li = st_ref[pl.ds(j, 1), ns:]
        st_ref[pl.ds(j, 1), :ns] = sr
        st_ref[pl.ds(j, 1), ns:] = si
        return (a8r * sr - a8i * si + lr, a8r * si + a8i * sr + li)

    sr, si = lax.fori_loop(0, nj, step, (carry_ref[0:1, :], carry_ref[1:2, :]), unroll=8)
    carry_ref[0:1, :] = sr
    carry_ref[1:2, :] = si

    stb = st_ref[...].astype(BF16)
    for kt in range(SSM_KT):
        y = (_dot(xall[kt], toep_ref[kt]) + _dot(stb[:, kt * gw:(kt + 1) * gw], cre_ref[kt])
             + _dot(stb[:, ns + kt * gw:ns + (kt + 1) * gw], cim_ref[kt]))
        for t in range(SSM_SUB):
            slab_ref[kt, pl.ds(t, nj, stride=SSM_SUB), :] = y[:, t * BLK:(t + 1) * BLK]
    y = jnp.concatenate([slab_ref[kt] for kt in range(SSM_KT)], axis=1)
    y = _gelu(y + d_ref[...] * x_ref[...].astype(F32))
    o_ref[...] = (y * _sigmoid(_dot(y.astype(BF16), wg_ref[...]) + bg_ref[...])).astype(o_ref.dtype)


MIX_BLOCKS = 8


def _mixers_kernel(q_ref, k_ref, v_ref, kp_ref, vp_ref, qg_ref, kg_ref, bias_ref, sink_ref,
                   u_ref, sv_ref, lng_ref, lnb_ref, ws_ref, bs_ref,
                   x_ref, toep_ref, w2_ref, cre_ref, cim_ref, a8_ref, d_ref, wg_ref, bg_ref,
                   ya_ref, ysg_ref, yssm_ref, slab_ref, st_ref, carry_ref, *, blocks_per_seq):
    _ssm_kernel(x_ref, toep_ref, w2_ref, cre_ref, cim_ref, a8_ref, d_ref, wg_ref, bg_ref, yssm_ref,
                slab_ref, st_ref, carry_ref, tiles_per_seq=blocks_per_seq // MIX_BLOCKS)
    _attn_kernel(q_ref, k_ref, v_ref, kp_ref, vp_ref, qg_ref, kg_ref, bias_ref, sink_ref, ya_ref,
                 blocks_per_step=MIX_BLOCKS, blocks_per_seq=blocks_per_seq)
    _sg_kernel(u_ref, sv_ref, lng_ref, lnb_ref, ws_ref, bs_ref, ysg_ref, chunks_per_step=MIX_BLOCKS)


def _mixers(proj, layer, seq_len, q_gain, k_gain, sinks, sg_ln_g, sg_ln_b, sg_w, sg_b,
            ssm_ops, d_skip, w_glu, b_glu):
    n = proj.shape[0]
    tm = MIX_BLOCKS * BLK
    nj = tm // SSM_SUB
    toep, w2, cre, cim, a8 = ssm_ops
    slopes = 2.0 ** (-8.0 * jnp.arange(1, N_Q_HEADS + 1, dtype=F32) / N_Q_HEADS)
    ri = jnp.arange(BLK)[:, None]
    ci = jnp.arange(BLK)[None, :]
    dist = jnp.where(ci <= ri, ri - ci, ri - ci + BLK).astype(F32)
    bias = (slopes[:, None, None] * dist).reshape(N_KV_HEADS, Q_PER_KV * BLK, BLK)
    sink_tile = jnp.broadcast_to(sinks.astype(F32)[:, None, None], (N_Q_HEADS, BLK, BLK))
    sink_tile = sink_tile.reshape(N_KV_HEADS, Q_PER_KV * BLK, BLK)

    def cols(width, off):
        return pl.BlockSpec((tm, width), lambda i: (i, off // width))

    def prev_block(off):
        return pl.BlockSpec((BLK, KV_W), lambda i: (jnp.maximum(i * MIX_BLOCKS - 1, 0), off // KV_W))

    def const(arr):
        return pl.BlockSpec(arr.shape, lambda i: (0,) * arr.ndim)

    def layer_slab(arr):
        nd = arr.ndim - 1
        return pl.BlockSpec((None,) + arr.shape[1:], lambda i: (layer,) + (0,) * nd,
                            pipeline_mode=pl.Buffered(1))

    q_off = GATE_W
    k_off, v_off = q_off + ATTN_W, q_off + ATTN_W + KV_W
    u_off, sv_off, x_off = v_off + KV_W, v_off + KV_W + SG_W, v_off + KV_W + 2 * SG_W
    attn_consts = (q_gain.reshape(1, HEAD_DIM).astype(F32), k_gain.reshape(1, HEAD_DIM).astype(F32),
                   bias, sink_tile)
    sg_consts = (sg_ln_g.reshape(1, SG_W).astype(F32), sg_ln_b.reshape(1, SG_W).astype(F32),
                 sg_w.astype(F32), sg_b.astype(F32)[..., None])
    ssm_consts = (toep, w2, cre, cim, a8, d_skip, w_glu, b_glu)
    row = lambda w: pl.BlockSpec((tm, w), lambda i: (i, 0))
    return pl.pallas_call(
        functools.partial(_mixers_kernel, blocks_per_seq=seq_len // BLK),
        grid=(n // tm,),
        in_specs=[cols(ATTN_W, q_off), cols(KV_W, k_off), cols(KV_W, v_off), prev_block(k_off), prev_block(v_off)]
                 + [const(a) for a in attn_consts]
                 + [cols(SG_W, u_off), cols(SG_W, sv_off)] + [const(a) for a in sg_consts]
                 + [cols(SSM_W, x_off)] + [layer_slab(a) for a in ssm_consts],
        out_specs=[row(ATTN_W), row(SG_W), row(SSM_W)],
        out_shape=[jax.ShapeDtypeStruct((n, w), BF16) for w in (ATTN_W, SG_W, SSM_W)],
        scratch_shapes=[pltpu.VMEM((SSM_KT, tm, BLK), F32),
                        pltpu.VMEM((nj, 2 * SSM_NSTATE), F32),
                        pltpu.VMEM((2, SSM_NSTATE), F32)],
        compiler_params=_cparams(("arbitrary",)),
        name="mixers",
    )(proj, proj, proj, proj, proj, *attn_consts, proj, proj, *sg_consts, proj, *ssm_consts)


def _merge_kernel(x_ref, g0_ref, g1_ref, g2_ref, ya_ref, ys_ref, ym_ref,
                  bg_ref, wa_ref, ws_ref, wm_ref, wo_ref, n2_ref, xo_ref, h_ref):
    d = D_MODEL

    def gate(g_ref, b):
        return _sigmoid(g_ref[...].astype(F32) + bg_ref[:, b * d:(b + 1) * d])

    merged = gate(g0_ref, 0) * _dot(ya_ref[...], wa_ref[...])
    merged = merged + gate(g1_ref, 1) * _dot(ys_ref[...], ws_ref[...])
    merged = merged + gate(g2_ref, 2) * _dot(ym_ref[...], wm_ref[...])
    xn = x_ref[...] + _dot(merged.astype(BF16), wo_ref[...])
    xo_ref[...] = xn
    h_ref[...] = (xn * _row_inv_rms(xn) * n2_ref[...]).astype(h_ref.dtype)


def _merge(x, proj, y_attn, y_sg, y_ssm, b_gate, wa, ws, wm, wo, norm2_g, layer, tm=512):
    n, d = x.shape
    row = lambda w: pl.BlockSpec((tm, w), lambda i: (i, 0))
    gate = lambda b: pl.BlockSpec((tm, d), lambda i: (i, b))

    def res(arr):
        nd = arr.ndim - 1
        return pl.BlockSpec((None,) + arr.shape[1:], lambda i: (layer,) + (0,) * nd,
                            pipeline_mode=pl.Buffered(1))

    return pl.pallas_call(
        _merge_kernel,
        grid=(n // tm,),
        in_specs=[row(d)] + [gate(b) for b in range(3)]
                 + [row(ATTN_W), row(SG_W), row(SSM_W),
                    res(b_gate), res(wa), res(ws), res(wm), res(wo), res(norm2_g)],
        out_specs=[row(d), row(d)],
        out_shape=[jax.ShapeDtypeStruct((n, d), F32), jax.ShapeDtypeStruct((n, d), BF16)],
        compiler_params=_cparams(("parallel",), MERGE_VMEM_LIMIT),
        name="merge_outproj",
    )(x, proj, proj, proj, y_attn, y_sg, y_ssm, b_gate, wa, ws, wm, wo, norm2_g)


def _ffn_up_kernel(h_ref, wg_ref, wv_ref, cwg_ref, cwv_ref, cbg_ref, cbv_ref, o_ref,
                   wb_ref, e_ref, *, tiles_per_seq, row_chunks):
    i = pl.program_id(1)
    tm = h_ref.shape[0]
    tn = o_ref.shape[1]
    tc = tm // row_chunks
    halo = F32_SUBLANES
    nslab = tn // BLK

    @pl.when(i == 0)
    def _():
        wb_ref[:, :tn] = wg_ref[...].astype(BF16)
        wb_ref[:, tn:] = wv_ref[...].astype(BF16)

    @pl.when(i % tiles_per_seq == 0)
    def _():
        e_ref[:, 0:halo, :] = jnp.zeros((2 * nslab, halo, BLK), F32)

    def conv(slab, base, cw_ref, cb_ref, lanes):
        return (cb_ref[:, lanes] + cw_ref[0:1, lanes] * e_ref[slab, base:base + tc, :]
                + cw_ref[1:2, lanes] * e_ref[slab, base - 1:base - 1 + tc, :]
                + cw_ref[2:3, lanes] * e_ref[slab, base - 2:base - 2 + tc, :])

    for r in range(row_chunks):
        rows = slice(r * tc, (r + 1) * tc)
        base = halo + r * tc
        e = _dot(h_ref[rows, :], wb_ref[...])
        for s in range(2 * nslab):
            e_ref[s, base:base + tc, :] = e[:, s * BLK:(s + 1) * BLK]
        for s in range(nslab):
            lanes = slice(s * BLK, (s + 1) * BLK)
            gate = conv(s, base, cwg_ref, cbg_ref, lanes)
            val = conv(nslab + s, base, cwv_ref, cbv_ref, lanes)
            o_ref[rows, lanes] = (_gelu(gate) * val).astype(o_ref.dtype)
    e_ref[:, 0:halo, :] = e_ref[:, tm:tm + halo, :]


def _ffn_up(h, w_up, conv_w, conv_b, layer, seq_len, tm=2048, tn=512, row_chunks=2):
    n, d = h.shape
    nj = D_FF // tn
    wspec = lambda off: pl.BlockSpec((None, d, tn), lambda j, i: (layer, 0, j + off))
    cspec = lambda rows, off: pl.BlockSpec((None, rows, tn), lambda j, i: (layer, 0, j + off))
    return pl.pallas_call(
        functools.partial(_ffn_up_kernel, tiles_per_seq=seq_len // tm, row_chunks=row_chunks),
        grid=(nj, n // tm),
        in_specs=[pl.BlockSpec((tm, d), lambda j, i: (i, 0)),
                  wspec(0), wspec(nj), cspec(3, 0), cspec(3, nj), cspec(1, 0), cspec(1, nj)],
        out_specs=pl.BlockSpec((tm, tn), lambda j, i: (i, j)),
        out_shape=jax.ShapeDtypeStruct((n, D_FF), BF16),
        scratch_shapes=[pltpu.VMEM((d, 2 * tn), BF16),
                        pltpu.VMEM((2 * tn // BLK, F32_SUBLANES + tm, BLK), F32)],
        compiler_params=_cparams(("parallel", "arbitrary")),
        name="ffn_up_conv",
    )(h, w_up, w_up, conv_w, conv_w, conv_b, conv_b)


def _ffn_down_kernel(a_ref, w_ref, x_ref, xo_ref):
    xo_ref[...] = x_ref[...] + _dot(a_ref[...], w_ref[...])


def _ffn_down_norm_kernel(a_ref, w_ref, x_ref, g_ref, xo_ref, h_ref, inv_ref, ssq_ref):
    j = pl.program_id(1)

    @pl.when(j == 0)
    def _():
        ssq_ref[...] = jnp.zeros_like(ssq_ref)

    half = x_ref.shape[0] // 2
    for r in range(2):
        rows = slice(r * half, (r + 1) * half)
        xn = x_ref[rows, :] + _dot(a_ref[rows, :], w_ref[...])
        xo_ref[rows, :] = xn
        h_ref[rows, :] = (xn * g_ref[...]).astype(h_ref.dtype)
        sq = xn * xn
        part = sq[:, 0:BLK]
        for c in range(1, sq.shape[1] // BLK):
            part = part + sq[:, c * BLK:(c + 1) * BLK]
        ssq_ref[rows, :] += part

    @pl.when(j == pl.num_programs(1) - 1)
    def _():
        ssq = jnp.sum(ssq_ref[...], axis=-1, keepdims=True)
        inv_ref[...] = lax.rsqrt(ssq * (1.0 / D_MODEL) + EPS)


FFN_DOWN_TN = 512


def _cast_kernel(w_ref, o_ref):
    o_ref[...] = w_ref[...].astype(o_ref.dtype)


def _ffn_down_weights(w_down, row_blocks=4):
    nl, k, d = w_down.shape
    tn = FFN_DOWN_TN
    tk = k // row_blocks
    return pl.pallas_call(
        _cast_kernel,
        grid=(nl, d // tn, row_blocks),
        in_specs=[pl.BlockSpec((None, tk, tn), lambda l, j, r: (l, r, j))],
        out_specs=pl.BlockSpec((None, None, tk, tn), lambda l, j, r: (l, j, r, 0)),
        out_shape=jax.ShapeDtypeStruct((nl, d // tn, k, tn), BF16),
        compiler_params=_cparams(("parallel", "parallel", "parallel")),
        name="ffn_down_weight_tiles",
    )(w_down)


def _ffn_down(act, w_down, x, norm_g, layer, next_layer, tm=1024):
    n, d = x.shape
    k = act.shape[1]
    tn = FFN_DOWN_TN
    tile = pl.BlockSpec((tm, tn), lambda i, j: (i, j))
    in_specs = [pl.BlockSpec((tm, k), lambda i, j: (i, 0)),
                pl.BlockSpec((None, None, k, tn), lambda i, j: (layer, j, 0, 0)),
                tile]
    common = dict(grid=(n // tm, d // tn), compiler_params=_cparams(("parallel", "arbitrary")))
    x_shape = jax.ShapeDtypeStruct((n, d), F32)
    if next_layer is None:
        xo = pl.pallas_call(_ffn_down_kernel, in_specs=in_specs, out_specs=tile, out_shape=x_shape,
                            name="ffn_down_last", **common)(act, w_down, x)
        return xo, None, None
    return pl.pallas_call(
        _ffn_down_norm_kernel,
        in_specs=in_specs + [pl.BlockSpec((None, 1, tn), lambda i, j: (next_layer, 0, j))],
        out_specs=[tile, tile, pl.BlockSpec((tm, 1), lambda i, j: (i, 0))],
        out_shape=[x_shape, jax.ShapeDtypeStruct((n, d), BF16), jax.ShapeDtypeStruct((n, 1), F32)],
        scratch_shapes=[pltpu.VMEM((tm, BLK), F32)],
        name="ffn_down",
        **common,
    )(act, w_down, x, norm_g)


def kernel(x, norm1_g, w_in, b_gate, q_norm_g, k_norm_g, attn_sinks, sg_ln_g, sg_ln_b, sg_w, sg_b, ssm_a_re, ssm_a_im, ssm_log_dt, ssm_b_re, ssm_b_im, ssm_c_re, ssm_c_im, ssm_d, ssm_w_glu, ssm_b_glu, w_proj_attn, w_proj_sg, w_proj_ssm, w_out, norm2_g, ffn_w_up, ffn_conv_w, ffn_conv_b, ffn_w_down):
    b, seq_len, d = x.shape
    depth = w_in.shape[0]
    xr = x.reshape(b * seq_len, d)
    ssm_ops = _ssm_operators(ssm_a_re, ssm_a_im, ssm_log_dt, ssm_b_re, ssm_b_im, ssm_c_re, ssm_c_im)
    w_glu, wa, ws, wm, wo = (w.astype(BF16) for w in (
        ssm_w_glu, w_proj_attn, w_proj_sg, w_proj_ssm, w_out))
    w_down = _ffn_down_weights(ffn_w_down)
    rowvec = lambda v: v.astype(F32).reshape(depth, 1, -1)
    ssm_d_r, b_glu_r, b_gate_r, norm1_r, norm2_r, conv_b_r = (rowvec(v) for v in (
        ssm_d, ssm_b_glu, b_gate, norm1_g, norm2_g, ffn_conv_b))

    h, inv = _rmsnorm(xr, norm1_r, 0)
    for l in range(depth):
        proj = _in_proj(h, inv, w_in, l)
        y_attn, y_sg, y_ssm = _mixers(proj, l, seq_len, q_norm_g[l], k_norm_g[l], attn_sinks[l],
                                      sg_ln_g[l], sg_ln_b[l], sg_w[l], sg_b[l],
                                      ssm_ops, ssm_d_r, w_glu, b_glu_r)
        xr, h2 = _merge(xr, proj, y_attn, y_sg, y_ssm, b_gate_r, wa, ws, wm, wo, norm2_r, l)
        act = _ffn_up(h2, ffn_w_up, ffn_conv_w, conv_b_r, l, seq_len)
        xr, h, inv = _ffn_down(act, w_down, xr, norm1_r, l, l + 1 if l + 1 < depth else None)
    return xr.reshape(b, seq_len, d)
```

```python
import functools
import math

import jax
import jax.numpy as jnp
from jax import lax
from jax.experimental import pallas as pl
from jax.experimental.pallas import tpu as pltpu

F32 = jnp.float32
BF16 = jnp.bfloat16

D_MODEL = 2048
N_Q_HEADS = 8
N_KV_HEADS = 2
Q_PER_KV = N_Q_HEADS // N_KV_HEADS
HEAD_DIM = 128
BLK = 128
ATTN_W = N_Q_HEADS * HEAD_DIM
KV_W = N_KV_HEADS * HEAD_DIM
SG_GROUPS = 4
SG_W = 512
SSM_GROUPS = 32
SSM_CH = 16
SSM_W = 512
SSM_STATE = 64
SSM_SUB = 8
SSM_KT = 4
SSM_NSTATE = SSM_GROUPS * SSM_STATE
MAIN_W = ATTN_W + 2 * KV_W + 2 * SG_W + SSM_W
GATE_W = 3 * D_MODEL
IN_W = MAIN_W + GATE_W
D_FF = 5632
F32_SUBLANES = 8
EPS = 1e-6

VMEM_LIMIT = 56 * 1024 * 1024
MERGE_VMEM_LIMIT = 58 * 1024 * 1024


def _cparams(sem, vmem_limit=VMEM_LIMIT):
    return pltpu.CompilerParams(dimension_semantics=sem, vmem_limit_bytes=vmem_limit)


def _gelu(x):
    c = math.sqrt(2.0 / math.pi)
    return 0.5 * x * (1.0 + jnp.tanh(c * (x + 0.044715 * (x * x * x))))


def _sigmoid(x):
    return 1.0 / (1.0 + jnp.exp(-x))


def _dot(a, b):
    return jnp.dot(a, b, preferred_element_type=F32)


def _row_inv_rms(x):
    return lax.rsqrt(jnp.mean(x * x, axis=-1, keepdims=True) + EPS)


def _rmsnorm_kernel(x_ref, g_ref, o_ref, inv_ref):
    x = x_ref[...]
    o_ref[...] = (x * g_ref[...]).astype(o_ref.dtype)
    inv_ref[...] = _row_inv_rms(x)


def _rmsnorm(x, g, layer, tm=1024):
    n, d = x.shape
    return pl.pallas_call(
        _rmsnorm_kernel,
        grid=(n // tm,),
        in_specs=[pl.BlockSpec((tm, d), lambda i: (i, 0)),
                  pl.BlockSpec((None, 1, d), lambda i: (layer, 0, 0))],
        out_specs=[pl.BlockSpec((tm, d), lambda i: (i, 0)), pl.BlockSpec((tm, 1), lambda i: (i, 0))],
        out_shape=[jax.ShapeDtypeStruct((n, d), BF16), jax.ShapeDtypeStruct((n, 1), F32)],
        compiler_params=_cparams(("parallel",)),
        name="rmsnorm",
    )(x, g)


def _in_proj_kernel(h_ref, inv_ref, w_ref, o_ref, wb_ref):
    @pl.when(pl.program_id(1) == 0)
    def _():
        wb_ref[...] = w_ref[...].astype(BF16)

    o_ref[...] = (_dot(h_ref[...], wb_ref[...]) * inv_ref[...]).astype(o_ref.dtype)


def _in_proj(h, inv, w_in, layer, tm=1024, tn=1536):
    n, d = h.shape
    width = w_in.shape[2]
    nj = width // tn
    main_tiles = MAIN_W // tn
    return pl.pallas_call(
        _in_proj_kernel,
        grid=(nj, n // tm),
        in_specs=[pl.BlockSpec((tm, d), lambda j, i: (i, 0)),
                  pl.BlockSpec((tm, 1), lambda j, i: (i, 0)),
                  pl.BlockSpec((None, d, tn), lambda j, i: (layer, 0, (j + main_tiles) % nj))],
        out_specs=pl.BlockSpec((tm, tn), lambda j, i: (i, j)),
        out_shape=jax.ShapeDtypeStruct((n, width), BF16),
        scratch_shapes=[pltpu.VMEM((d, tn), BF16)],
        compiler_params=_cparams(("parallel", "arbitrary")),
        name="in_proj",
    )(h, inv, w_in)


def _attn_kernel(q_ref, k_ref, v_ref, kp_ref, vp_ref, qg_ref, kg_ref, bias_ref, sink_ref, o_ref,
                 *, blocks_per_step, blocks_per_seq):
    i = pl.program_id(0)
    qg = qg_ref[...] * (HEAD_DIM ** -0.5)
    kg = kg_ref[...]
    row = lax.broadcasted_iota(jnp.int32, (BLK, BLK), 0)
    col = lax.broadcasted_iota(jnp.int32, (BLK, BLK), 1)
    cur4 = jnp.concatenate([col <= row] * Q_PER_KV, axis=0)
    ones = jnp.ones((BLK, BLK), BF16)
    ones_win = jnp.ones((2 * BLK, BLK), BF16)

    def rms(x, g):
        ssq = _dot((x * x).astype(BF16), ones)
        return x * lax.rsqrt(ssq * (1.0 / HEAD_DIM) + EPS) * g

    k_prev = None
    for r in range(blocks_per_step):
        rows = slice(r * BLK, (r + 1) * BLK)
        if r == 0:
            has_prev = (i * blocks_per_step) % blocks_per_seq != 0
            k_prev = [rms(kp_ref[:, h * HEAD_DIM:(h + 1) * HEAD_DIM].astype(F32), kg)
                      for h in range(N_KV_HEADS)]
            v_prev = vp_ref[...]
        else:
            has_prev = None
            v_prev = v_ref[(r - 1) * BLK:r * BLK, :]
        k_cur = []
        for h in range(N_KV_HEADS):
            hc = slice(h * HEAD_DIM, (h + 1) * HEAD_DIM)
            k_cur.append(rms(k_ref[rows, hc].astype(F32), kg))
            kwin = jnp.concatenate([k_prev[h], k_cur[h]], axis=0).astype(BF16)
            vwin = jnp.concatenate([v_prev[:, hc], v_ref[rows, hc]], axis=0)
            vext = jnp.concatenate([vwin, ones_win], axis=1)
            q4 = jnp.concatenate(
                [q_ref[rows, (h * Q_PER_KV + g) * HEAD_DIM:(h * Q_PER_KV + g + 1) * HEAD_DIM].astype(F32)
                 for g in range(Q_PER_KV)], axis=0)
            q4 = rms(q4, qg).astype(BF16)
            s2 = lax.dot_general(q4, kwin, (((1,), (1,)), ((), ())),
                                 preferred_element_type=F32)
            s = jnp.where(cur4, s2[:, BLK:], s2[:, :BLK]) - bias_ref[h]
            if has_prev is not None:
                s = jnp.where(jnp.logical_or(cur4, has_prev), s, -jnp.inf)
            sink = sink_ref[h]
            m = jnp.maximum(jnp.max(s, axis=-1, keepdims=True), sink)
            p = jnp.exp(s - m)
            p2 = jnp.concatenate([jnp.where(cur4, 0.0, p), jnp.where(cur4, p, 0.0)],
                                 axis=1).astype(BF16)
            ov = _dot(p2, vext)
            o = ov[:, :BLK] / (ov[:, BLK:] + jnp.exp(sink - m))
            for g in range(Q_PER_KV):
                c0 = (h * Q_PER_KV + g) * HEAD_DIM
                o_ref[rows, c0:c0 + HEAD_DIM] = o[g * BLK:(g + 1) * BLK].astype(o_ref.dtype)
        k_prev = k_cur


def _sg_kernel(u_ref, v_ref, lng_ref, lnb_ref, w_ref, b_ref, o_ref, *, chunks_per_step):
    row = lax.broadcasted_iota(jnp.int32, (BLK, BLK), 0)
    col = lax.broadcasted_iota(jnp.int32, (BLK, BLK), 1)
    tril = col <= row
    for c in range(chunks_per_step):
        rows = slice(c * BLK, (c + 1) * BLK)
        for g in range(SG_GROUPS):
            gc = slice(g * BLK, (g + 1) * BLK)
            zv = _gelu(v_ref[rows, gc].astype(F32))
            mu = jnp.mean(zv, axis=-1, keepdims=True)
            zc = zv - mu
            var = jnp.mean(zc * zc, axis=-1, keepdims=True)
            zn = zc * lax.rsqrt(var + EPS) * lng_ref[:, gc] + lnb_ref[:, gc]
            w = jnp.where(tril, w_ref[g], 0.0).astype(BF16)
            mixed = _dot(w, zn.astype(BF16)) + b_ref[g]
            zu = _gelu(u_ref[rows, gc].astype(F32))
            o_ref[rows, gc] = (zu * mixed).astype(o_ref.dtype)


def _ssm_operators(a_re, a_im, log_dt, b_re, b_im, c_re, c_im):
    hp = lax.Precision.HIGHEST
    p, c, kt, sub = SSM_STATE, SSM_CH, SSM_KT, SSM_SUB
    gl = SSM_GROUPS // kt
    nl = a_re.shape[0]
    a_re, a_im = a_re.astype(F32)[:, None], a_im.astype(F32)[:, None]
    dt = jnp.exp(log_dt.astype(F32))[:, None, :, None]
    k = jnp.arange(sub + 1, dtype=F32)[None, :, None, None]
    mag = jnp.exp(a_re * dt * k)
    pr, pi = mag * jnp.cos(a_im * dt * k), mag * jnp.sin(a_im * dt * k)
    ar, ai, lr, li = pr[:, 1], pi[:, 1], a_re[:, 0], a_im[:, 0]
    den = lr * lr + li * li
    cr = ((ar - 1.0) * lr + ai * li) / den
    ci = (ai * lr - (ar - 1.0) * li) / den
    bbr = cr[..., None] * b_re - ci[..., None] * b_im
    bbi = cr[..., None] * b_im + ci[..., None] * b_re
    bbr_t, bbi_t = jnp.swapaxes(bbr, -1, -2), jnp.swapaxes(bbi, -1, -2)
    car = c_re[:, None] * pr[:, :, :, None, :] - c_im[:, None] * pi[:, :, :, None, :]
    cai = c_re[:, None] * pi[:, :, :, None, :] + c_im[:, None] * pr[:, :, :, None, :]

    def expand(x, rows_per_group, blocks, width):
        col = jnp.arange(blocks * gl * width)
        src = (col // (gl * width)) * width + col % width
        spread = (jnp.arange(blocks * width)[:, None] == src[None, :]).astype(BF16)
        row_group = (jnp.arange(x.shape[2]) // rows_per_group) % gl
        mask = (row_group[:, None] == ((col // width) % gl)[None, :]).astype(BF16)
        return jnp.einsum("xkrm,mn->xkrn", x.astype(BF16), spread, preferred_element_type=BF16) * mask

    kk = (jnp.einsum("xgdp,xlgcp->xlgdc", bbr_t, car[:, :sub], precision=hp)
          - jnp.einsum("xgdp,xlgcp->xlgdc", bbi_t, cai[:, :sub], precision=hp))
    krow = kk.reshape(nl, sub, kt, gl * c, c).transpose(0, 2, 3, 1, 4).reshape(nl, kt, gl * c, sub * c)
    krow = jnp.pad(krow, ((0, 0), (0, 0), (0, 0), (sub * c, 0)))
    kst = jnp.concatenate([krow[..., (sub - s) * c:(2 * sub - s) * c] for s in range(sub)], axis=2)
    toep = expand(kst, c, sub, c)

    rr, ri = pr[:, sub - 1::-1][:, :, :, None, :], pi[:, sub - 1::-1][:, :, :, None, :]
    wr = rr * bbr_t[:, None] - ri * bbi_t[:, None]
    wi = rr * bbi_t[:, None] + ri * bbr_t[:, None]

    def inject(x):
        x = x.reshape(nl, sub, kt, gl * c, p).transpose(0, 2, 1, 3, 4)
        return expand(x.reshape(nl, kt, sub * gl * c, p), c, 1, p)

    w2 = jnp.concatenate([inject(wr), inject(wi)], axis=-1)

    def readout(x):
        x = x.reshape(nl, sub, kt, gl, c, p).transpose(0, 2, 3, 5, 1, 4)
        return expand(x.reshape(nl, kt, gl * p, sub * c), p, sub, c)

    cre = readout(car[:, 1:])
    cim = readout(-cai[:, 1:])
    a8 = jnp.stack([pr[:, sub].reshape(nl, -1), pi[:, sub].reshape(nl, -1)], axis=1)
    return toep, w2, cre, cim, a8


def _ssm_kernel(x_ref, toep_ref, w2_ref, cre_ref, cim_ref, a8_ref, d_ref, wg_ref, bg_ref, o_ref,
                slab_ref, st_ref, carry_ref, *, tiles_per_seq):
    i = pl.program_id(0)
    tm = x_ref.shape[0]
    nj = tm // SSM_SUB
    ns = SSM_NSTATE
    gw = ns // SSM_KT

    @pl.when(i % tiles_per_seq == 0)
    def _():
        carry_ref[...] = jnp.zeros_like(carry_ref)

    xall = []
    for kt in range(SSM_KT):
        slab_ref[kt] = x_ref[:, kt * BLK:(kt + 1) * BLK].astype(F32)
        xall.append(jnp.concatenate(
            [slab_ref[kt, pl.ds(s, nj, stride=SSM_SUB), :].astype(BF16) for s in range(SSM_SUB)], axis=1))

    for kt in range(SSM_KT):
        acc = _dot(xall[kt], w2_ref[kt])
        st_ref[:, kt * gw:(kt + 1) * gw] = acc[:, :gw]
        st_ref[:, ns + kt * gw:ns + (kt + 1) * gw] = acc[:, gw:]

    a8r = a8_ref[0:1, :]
    a8i = a8_ref[1:2, :]

    def step(j, st):
        sr, si = st
        lr = st_ref[pl.ds(j, 1), :ns]
        li = st_ref[pl.ds(j, 1), ns:]
        st_ref[pl.ds(j, 1), :ns] = sr
        st_ref[pl.ds(j, 1), ns:] = si
        return (a8r * sr - a8i * si + lr, a8r * si + a8i * sr + li)

    sr, si = lax.fori_loop(0, nj, step, (carry_ref[0:1, :], carry_ref[1:2, :]), unroll=8)
    carry_ref[0:1, :] = sr
    carry_ref[1:2, :] = si

    stb = st_ref[...].astype(BF16)
    for kt in range(SSM_KT):
        y = (_dot(xall[kt], toep_ref[kt]) + _dot(stb[:, kt * gw:(kt + 1) * gw], cre_ref[kt])
             + _dot(stb[:, ns + kt * gw:ns + (kt + 1) * gw], cim_ref[kt]))
        for t in range(SSM_SUB):
            slab_ref[kt, pl.ds(t, nj, stride=SSM_SUB), :] = y[:, t * BLK:(t + 1) * BLK]
    y = jnp.concatenate([slab_ref[kt] for kt in range(SSM_KT)], axis=1)
    y = _gelu(y + d_ref[...] * x_ref[...].astype(F32))
    o_ref[...] = (y * _sigmoid(_dot(y.astype(BF16), wg_ref[...]) + bg_ref[...])).astype(o_ref.dtype)


MIX_BLOCKS = 8


def _mixers_kernel(q_ref, k_ref, v_ref, kp_ref, vp_ref, qg_ref, kg_ref, bias_ref, sink_ref,
                   u_ref, sv_ref, lng_ref, lnb_ref, ws_ref, bs_ref,
                   x_ref, toep_ref, w2_ref, cre_ref, cim_ref, a8_ref, d_ref, wg_ref, bg_ref,
                   ya_ref, ysg_ref, yssm_ref, slab_ref, st_ref, carry_ref, *, blocks_per_seq):
    _ssm_kernel(x_ref, toep_ref, w2_ref, cre_ref, cim_ref, a8_ref, d_ref, wg_ref, bg_ref, yssm_ref,
                slab_ref, st_ref, carry_ref, tiles_per_seq=blocks_per_seq // MIX_BLOCKS)
    _attn_kernel(q_ref, k_ref, v_ref, kp_ref, vp_ref, qg_ref, kg_ref, bias_ref, sink_ref, ya_ref,
                 blocks_per_step=MIX_BLOCKS, blocks_per_seq=blocks_per_seq)
    _sg_kernel(u_ref, sv_ref, lng_ref, lnb_ref, ws_ref, bs_ref, ysg_ref, chunks_per_step=MIX_BLOCKS)


def _mixers(proj, layer, seq_len, q_gain, k_gain, sinks, sg_ln_g, sg_ln_b, sg_w, sg_b,
            ssm_ops, d_skip, w_glu, b_glu):
    n = proj.shape[0]
    tm = MIX_BLOCKS * BLK
    nj = tm // SSM_SUB
    toep, w2, cre, cim, a8 = ssm_ops
    slopes = 2.0 ** (-8.0 * jnp.arange(1, N_Q_HEADS + 1, dtype=F32) / N_Q_HEADS)
    ri = jnp.arange(BLK)[:, None]
    ci = jnp.arange(BLK)[None, :]
    dist = jnp.where(ci <= ri, ri - ci, ri - ci + BLK).astype(F32)
    bias = (slopes[:, None, None] * dist).reshape(N_KV_HEADS, Q_PER_KV * BLK, BLK)
    sink_tile = jnp.broadcast_to(sinks.astype(F32)[:, None, None], (N_Q_HEADS, BLK, BLK))
    sink_tile = sink_tile.reshape(N_KV_HEADS, Q_PER_KV * BLK, BLK)

    def cols(width, off):
        return pl.BlockSpec((tm, width), lambda i: (i, off // width))

    def prev_block(off):
        return pl.BlockSpec((BLK, KV_W), lambda i: (jnp.maximum(i * MIX_BLOCKS - 1, 0), off // KV_W))

    def const(arr):
        return pl.BlockSpec(arr.shape, lambda i: (0,) * arr.ndim)

    def layer_slab(arr):
        nd = arr.ndim - 1
        return pl.BlockSpec((None,) + arr.shape[1:], lambda i: (layer,) + (0,) * nd,
                            pipeline_mode=pl.Buffered(1))

    q_off = GATE_W
    k_off, v_off = q_off + ATTN_W, q_off + ATTN_W + KV_W
    u_off, sv_off, x_off = v_off + KV_W, v_off + KV_W + SG_W, v_off + KV_W + 2 * SG_W
    attn_consts = (q_gain.reshape(1, HEAD_DIM).astype(F32), k_gain.reshape(1, HEAD_DIM).astype(F32),
                   bias, sink_tile)
    sg_consts = (sg_ln_g.reshape(1, SG_W).astype(F32), sg_ln_b.reshape(1, SG_W).astype(F32),
                 sg_w.astype(F32), sg_b.astype(F32)[..., None])
    ssm_consts = (toep, w2, cre, cim, a8, d_skip, w_glu, b_glu)
    row = lambda w: pl.BlockSpec((tm, w), lambda i: (i, 0))
    return pl.pallas_call(
        functools.partial(_mixers_kernel, blocks_per_seq=seq_len // BLK),
        grid=(n // tm,),
        in_specs=[cols(ATTN_W, q_off), cols(KV_W, k_off), cols(KV_W, v_off), prev_block(k_off), prev_block(v_off)]
                 + [const(a) for a in attn_consts]
                 + [cols(SG_W, u_off), cols(SG_W, sv_off)] + [const(a) for a in sg_consts]
                 + [cols(SSM_W, x_off)] + [layer_slab(a) for a in ssm_consts],
        out_specs=[row(ATTN_W), row(SG_W), row(SSM_W)],
        out_shape=[jax.ShapeDtypeStruct((n, w), BF16) for w in (ATTN_W, SG_W, SSM_W)],
        scratch_shapes=[pltpu.VMEM((SSM_KT, tm, BLK), F32),
                        pltpu.VMEM((nj, 2 * SSM_NSTATE), F32),
                        pltpu.VMEM((2, SSM_NSTATE), F32)],
        compiler_params=_cparams(("arbitrary",)),
        name="mixers",
    )(proj, proj, proj, proj, proj, *attn_consts, proj, proj, *sg_consts, proj, *ssm_consts)


def _merge_kernel(x_ref, g0_ref, g1_ref, g2_ref, ya_ref, ys_ref, ym_ref,
                  bg_ref, wa_ref, ws_ref, wm_ref, wo_ref, n2_ref, xo_ref, h_ref):
    d = D_MODEL

    half = x_ref.shape[0] // 2
    for r in range(2):
        rows = slice(r * half, (r + 1) * half)

        def gate(g_ref, b):
            return _sigmoid(g_ref[rows, :].astype(F32) + bg_ref[:, b * d:(b + 1) * d])

        merged = gate(g0_ref, 0) * _dot(ya_ref[rows, :], wa_ref[...])
        merged = merged + gate(g1_ref, 1) * _dot(ys_ref[rows, :], ws_ref[...])
        merged = merged + gate(g2_ref, 2) * _dot(ym_ref[rows, :], wm_ref[...])
        xn = x_ref[rows, :] + _dot(merged.astype(BF16), wo_ref[...])
        xo_ref[rows, :] = xn
        h_ref[rows, :] = (xn * _row_inv_rms(xn) * n2_ref[...]).astype(h_ref.dtype)


def _merge(x, proj, y_attn, y_sg, y_ssm, b_gate, wa, ws, wm, wo, norm2_g, layer, tm=512):
    n, d = x.shape
    row = lambda w: pl.BlockSpec((tm, w), lambda i: (i, 0))
    gate = lambda b: pl.BlockSpec((tm, d), lambda i: (i, b))

    def res(arr):
        nd = arr.ndim - 1
        return pl.BlockSpec((None,) + arr.shape[1:], lambda i: (layer,) + (0,) * nd,
                            pipeline_mode=pl.Buffered(1))

    return pl.pallas_call(
        _merge_kernel,
        grid=(n // tm,),
        in_specs=[row(d)] + [gate(b) for b in range(3)]
                 + [row(ATTN_W), row(SG_W), row(SSM_W),
                    res(b_gate), res(wa), res(ws), res(wm), res(wo), res(norm2_g)],
        out_specs=[row(d), row(d)],
        out_shape=[jax.ShapeDtypeStruct((n, d), F32), jax.ShapeDtypeStruct((n, d), BF16)],
        compiler_params=_cparams(("parallel",), MERGE_VMEM_LIMIT),
        name="merge_outproj",
    )(x, proj, proj, proj, y_attn, y_sg, y_ssm, b_gate, wa, ws, wm, wo, norm2_g)


def _ffn_up_kernel(h_ref, wg_ref, wv_ref, cwg_ref, cwv_ref, cbg_ref, cbv_ref, o_ref,
                   wb_ref, e_ref, *, tiles_per_seq, row_chunks):
    i = pl.program_id(1)
    tm = h_ref.shape[0]
    tn = o_ref.shape[1]
    tc = tm // row_chunks
    halo = F32_SUBLANES
    nslab = tn // BLK

    @pl.when(i == 0)
    def _():
        wb_ref[:, :tn] = wg_ref[...].astype(BF16)
        wb_ref[:, tn:] = wv_ref[...].astype(BF16)

    @pl.when(i % tiles_per_seq == 0)
    def _():
        e_ref[:, 0:halo, :] = jnp.zeros((2 * nslab, halo, BLK), F32)

    def conv(slab, base, cw_ref, cb_ref, lanes):
        return (cb_ref[:, lanes] + cw_ref[0:1, lanes] * e_ref[slab, base:base + tc, :]
                + cw_ref[1:2, lanes] * e_ref[slab, base - 1:base - 1 + tc, :]
                + cw_ref[2:3, lanes] * e_ref[slab, base - 2:base - 2 + tc, :])

    for r in range(row_chunks):
        rows = slice(r * tc, (r + 1) * tc)
        base = halo + r * tc
        e = _dot(h_ref[rows, :], wb_ref[...])
        for s in range(2 * nslab):
            e_ref[s, base:base + tc, :] = e[:, s * BLK:(s + 1) * BLK]
        for s in range(nslab):
            lanes = slice(s * BLK, (s + 1) * BLK)
            gate = conv(s, base, cwg_ref, cbg_ref, lanes)
            val = conv(nslab + s, base, cwv_ref, cbv_ref, lanes)
            o_ref[rows, lanes] = (_gelu(gate) * val).astype(o_ref.dtype)
    e_ref[:, 0:halo, :] = e_ref[:, tm:tm + halo, :]


def _ffn_up(h, w_up, conv_w, conv_b, layer, seq_len, tm=2048, tn=512, row_chunks=2):
    n, d = h.shape
    nj = D_FF // tn
    wspec = lambda off: pl.BlockSpec((None, d, tn), lambda j, i: (layer, 0, j + off))
    cspec = lambda rows, off: pl.BlockSpec((None, rows, tn), lambda j, i: (layer, 0, j + off))
    return pl.pallas_call(
        functools.partial(_ffn_up_kernel, tiles_per_seq=seq_len // tm, row_chunks=row_chunks),
        grid=(nj, n // tm),
        in_specs=[pl.BlockSpec((tm, d), lambda j, i: (i, 0)),
                  wspec(0), wspec(nj), cspec(3, 0), cspec(3, nj), cspec(1, 0), cspec(1, nj)],
        out_specs=pl.BlockSpec((tm, tn), lambda j, i: (i, j)),
        out_shape=jax.ShapeDtypeStruct((n, D_FF), BF16),
        scratch_shapes=[pltpu.VMEM((d, 2 * tn), BF16),
                        pltpu.VMEM((2 * tn // BLK, F32_SUBLANES + tm, BLK), F32)],
        compiler_params=_cparams(("parallel", "arbitrary")),
        name="ffn_up_conv",
    )(h, w_up, w_up, conv_w, conv_w, conv_b, conv_b)


def _ffn_down_kernel(a_ref, w_ref, x_ref, xo_ref):
    xo_ref[...] = x_ref[...] + _dot(a_ref[...], w_ref[...])


def _ffn_down_norm_kernel(a_ref, w_ref, x_ref, g_ref, xo_ref, h_ref, inv_ref, ssq_ref):
    j = pl.program_id(1)

    @pl.when(j == 0)
    def _():
        ssq_ref[...] = jnp.zeros_like(ssq_ref)

    half = x_ref.shape[0] // 2
    for r in range(2):
        rows = slice(r * half, (r + 1) * half)
        xn = x_ref[rows, :] + _dot(a_ref[rows, :], w_ref[...])
        xo_ref[rows, :] = xn
        h_ref[rows, :] = (xn * g_ref[...]).astype(h_ref.dtype)
        sq = xn * xn
        part = sq[:, 0:BLK]
        for c in range(1, sq.shape[1] // BLK):
            part = part + sq[:, c * BLK:(c + 1) * BLK]
        ssq_ref[rows, :] += part

    @pl.when(j == pl.num_programs(1) - 1)
    def _():
        ssq = jnp.sum(ssq_ref[...], axis=-1, keepdims=True)
        inv_ref[...] = lax.rsqrt(ssq * (1.0 / D_MODEL) + EPS)


FFN_DOWN_TN = 512


def _cast_kernel(w_ref, o_ref):
    o_ref[...] = w_ref[...].astype(o_ref.dtype)


def _ffn_down_weights(w_down, row_blocks=4):
    nl, k, d = w_down.shape
    tn = FFN_DOWN_TN
    tk = k // row_blocks
    return pl.pallas_call(
        _cast_kernel,
        grid=(nl, d // tn, row_blocks),
        in_specs=[pl.BlockSpec((None, tk, tn), lambda l, j, r: (l, r, j))],
        out_specs=pl.BlockSpec((None, None, tk, tn), lambda l, j, r: (l, j, r, 0)),
        out_shape=jax.ShapeDtypeStruct((nl, d // tn, k, tn), BF16),
        compiler_params=_cparams(("parallel", "parallel", "parallel")),
        name="ffn_down_weight_tiles",
    )(w_down)


def _ffn_down(act, w_down, x, norm_g, layer, next_layer, tm=1024):
    n, d = x.shape
    k = act.shape[1]
    tn = FFN_DOWN_TN
    tile = pl.BlockSpec((tm, tn), lambda i, j: (i, j))
    in_specs = [pl.BlockSpec((tm, k), lambda i, j: (i, 0)),
                pl.BlockSpec((None, None, k, tn), lambda i, j: (layer, j, 0, 0)),
                tile]
    common = dict(grid=(n // tm, d // tn), compiler_params=_cparams(("parallel", "arbitrary")))
    x_shape = jax.ShapeDtypeStruct((n, d), F32)
    if next_layer is None:
        xo = pl.pallas_call(_ffn_down_kernel, in_specs=in_specs, out_specs=tile, out_shape=x_shape,
                            name="ffn_down_last", **common)(act, w_down, x)
        return xo, None, None
    return pl.pallas_call(
        _ffn_down_norm_kernel,
        in_specs=in_specs + [pl.BlockSpec((None, 1, tn), lambda i, j: (next_layer, 0, j))],
        out_specs=[tile, tile, pl.BlockSpec((tm, 1), lambda i, j: (i, 0))],
        out_shape=[x_shape, jax.ShapeDtypeStruct((n, d), BF16), jax.ShapeDtypeStruct((n, 1), F32)],
        scratch_shapes=[pltpu.VMEM((tm, BLK), F32)],
        name="ffn_down",
        **common,
    )(act, w_down, x, norm_g)


def kernel(x, norm1_g, w_in, b_gate, q_norm_g, k_norm_g, attn_sinks, sg_ln_g, sg_ln_b, sg_w, sg_b, ssm_a_re, ssm_a_im, ssm_log_dt, ssm_b_re, ssm_b_im, ssm_c_re, ssm_c_im, ssm_d, ssm_w_glu, ssm_b_glu, w_proj_attn, w_proj_sg, w_proj_ssm, w_out, norm2_g, ffn_w_up, ffn_conv_w, ffn_conv_b, ffn_w_down):
    b, seq_len, d = x.shape
    depth = w_in.shape[0]
    xr = x.reshape(b * seq_len, d)
    ssm_ops = _ssm_operators(ssm_a_re, ssm_a_im, ssm_log_dt, ssm_b_re, ssm_b_im, ssm_c_re, ssm_c_im)
    w_glu, wa, ws, wm, wo = (w.astype(BF16) for w in (
        ssm_w_glu, w_proj_attn, w_proj_sg, w_proj_ssm, w_out))
    w_down = _ffn_down_weights(ffn_w_down)
    rowvec = lambda v: v.astype(F32).reshape(depth, 1, -1)
    ssm_d_r, b_glu_r, b_gate_r, norm1_r, norm2_r, conv_b_r = (rowvec(v) for v in (
        ssm_d, ssm_b_glu, b_gate, norm1_g, norm2_g, ffn_conv_b))

    h, inv = _rmsnorm(xr, norm1_r, 0)
    for l in range(depth):
        proj = _in_proj(h, inv, w_in, l)
        y_attn, y_sg, y_ssm = _mixers(proj, l, seq_len, q_norm_g[l], k_norm_g[l], attn_sinks[l],
                                      sg_ln_g[l], sg_ln_b[l], sg_w[l], sg_b[l],
                                      ssm_ops, ssm_d_r, w_glu, b_glu_r)
        xr, h2 = _merge(xr, proj, y_attn, y_sg, y_ssm, b_gate_r, wa, ws, wm, wo, norm2_r, l)
        act = _ffn_up(h2, ffn_w_up, ffn_conv_w, conv_b_r, l, seq_len)
        xr, h, inv = _ffn_down(act, w_down, xr, norm1_r, l, l + 1 if l + 1 < depth else None)
    return xr.reshape(b, seq_len, d)
```
